```python
import math
import jax, jax.numpy as jnp
from jax import lax
import numpy as np

D_MODEL = 4096
BATCH = 4
SEQ = 4096
DEPTH = 2

HEAD_DIM = 128
BLOCK_Q = 128
EPS = 1e-6
D_FF = 11008
MIX_WIDTH = D_MODEL
CONV_WIDTH = 3
CONV_CH = D_MODEL // 4
DIFF_V_DIM = 2 * HEAD_DIM
DIFF_HEADS = (MIX_WIDTH - CONV_CH) // DIFF_V_DIM
DIFF_QK = 2 * DIFF_HEADS * HEAD_DIM
DIFF_WIDTH = DIFF_HEADS * DIFF_V_DIM
EVEN_IN = 3 * CONV_CH + 2 * DIFF_QK + DIFF_WIDTH
SGU_CHUNK = 128
SGU_WIDTH = D_MODEL // 4
SGU_GROUPS = SGU_WIDTH // HEAD_DIM
FOX_HEADS = (MIX_WIDTH - SGU_WIDTH) // HEAD_DIM
FOX_WIDTH = FOX_HEADS * HEAD_DIM
ODD_IN = 2 * SGU_WIDTH + 3 * FOX_WIDTH + FOX_HEADS
N_EVEN = (DEPTH + 1) // 2
N_ODD = DEPTH // 2

kernel_name = "hybrid_conv_diffattn_sgu_fox_macaron"


def rmsnorm(x, g):
    xf = x.astype(jnp.float32)
    y = xf * lax.rsqrt(jnp.mean(xf * xf, axis=-1, keepdims=True) + EPS)
    return (y * g.astype(jnp.float32)).astype(x.dtype)


def layernorm(x, g, b):
    xf = x.astype(jnp.float32)
    mu = jnp.mean(xf, axis=-1, keepdims=True)
    var = jnp.mean(jnp.square(xf - mu), axis=-1, keepdims=True)
    y = (xf - mu) * lax.rsqrt(var + EPS)
    return (y * g.astype(jnp.float32) + b.astype(jnp.float32)).astype(x.dtype)


def swiglu(x, w_in, w_out):
    gate, up = jnp.split(x @ w_in, 2, axis=-1)
    return (jax.nn.silu(gate) * up) @ w_out


def alibi_slopes(n):
    def pow2(m):
        start = 2.0 ** (-8.0 / m)
        return [start ** (i + 1) for i in range(m)]
    if math.log2(n).is_integer():
        s = pow2(n)
    else:
        c = 2 ** int(math.floor(math.log2(n)))
        s = pow2(c) + pow2(2 * c)[0::2][: n - c]
    return jnp.asarray(np.array(s, dtype=np.float32))


def to_blocks(a):
    b, s = a.shape[:2]
    return jnp.moveaxis(a.reshape(b, s // BLOCK_Q, BLOCK_Q, *a.shape[2:]), 1, 0)


def from_blocks(a):
    a = jnp.moveaxis(a, 0, 1)
    return a.reshape(a.shape[0], -1, *a.shape[3:])


def short_conv(z, w):
    c = z.shape[-1]
    return lax.conv_general_dilated(
        z, w[:, None, :].astype(z.dtype), window_strides=(1,),
        padding=[(CONV_WIDTH - 1, 0)], dimension_numbers=("NWC", "WIO", "NWC"),
        feature_group_count=c)


def diff_attention(q, k, v, lam, slopes):
    s = q.shape[1]
    kpos = jnp.arange(s)
    kf = k.astype(jnp.float32)
    scale = HEAD_DIM ** -0.5

    def block(args):
        qb, start = args
        qpos = start + jnp.arange(BLOCK_Q)
        dist = (qpos[:, None] - kpos[None, :]).astype(jnp.float32)
        logits = jnp.einsum("bqhmd,bkhmd->bhmqk", qb.astype(jnp.float32), kf) * scale
        logits = logits - slopes[None, :, None, None, None] * dist
        logits = jnp.where(dist >= 0, logits, -jnp.inf)
        p = jax.nn.softmax(logits, axis=-1)
        wgt = p[:, :, 0] - lam * p[:, :, 1]
        return jnp.einsum("bhqk,bkhe->bqhe", wgt.astype(v.dtype), v)

    starts = jnp.arange(s // BLOCK_Q) * BLOCK_Q
    return from_blocks(lax.map(block, (to_blocks(q), starts)))


def fox_attention(q, k, v, cum):
    s = q.shape[1]
    kpos = jnp.arange(s)
    kf = k.astype(jnp.float32)
    cum_k = jnp.transpose(cum, (0, 2, 1))
    scale = HEAD_DIM ** -0.5

    def block(args):
        qb, cb, start = args
        qpos = start + jnp.arange(BLOCK_Q)
        causal = qpos[:, None] >= kpos[None, :]
        logits = jnp.einsum("bqhd,bkhd->bhqk", qb.astype(jnp.float32), kf) * scale
        logits = logits + (jnp.transpose(cb, (0, 2, 1))[..., :, None] - cum_k[..., None, :])
        logits = jnp.where(causal, logits, -jnp.inf)
        p = jax.nn.softmax(logits, axis=-1)
        return jnp.einsum("bhqk,bkhd->bqhd", p.astype(v.dtype), v)

    starts = jnp.arange(s // BLOCK_Q) * BLOCK_Q
    return from_blocks(lax.map(block, (to_blocks(q), to_blocks(cum), starts)))


def even_mixer(h, w_in, w_out, conv_w, lam_p, subln_g, layer_idx):
    b, s = h.shape[:2]
    proj = h @ w_in
    cuts = [CONV_CH, 2 * CONV_CH, 3 * CONV_CH, 3 * CONV_CH + DIFF_QK, 3 * CONV_CH + 2 * DIFF_QK]
    gate_b, gate_c, xt, q, k, v = jnp.split(proj, cuts, axis=-1)
    y_a = gate_b * short_conv(gate_c * xt, conv_w)
    q = q.reshape(b, s, DIFF_HEADS, 2, HEAD_DIM)
    k = k.reshape(b, s, DIFF_HEADS, 2, HEAD_DIM)
    v = v.reshape(b, s, DIFF_HEADS, DIFF_V_DIM)
    lam_init = 0.8 - 0.6 * math.exp(-0.3 * layer_idx)
    lp = lam_p.astype(jnp.float32)
    lam = jnp.exp(jnp.sum(lp[0] * lp[1])) - jnp.exp(jnp.sum(lp[2] * lp[3])) + lam_init
    o = diff_attention(q, k, v, lam, alibi_slopes(DIFF_HEADS))
    o = rmsnorm(o, subln_g) * (1.0 - lam_init)
    y_b = o.reshape(b, s, DIFF_WIDTH).astype(y_a.dtype)
    return jnp.concatenate([y_a, y_b], axis=-1) @ w_out


def odd_mixer(h, w_in, w_out, sgu_w, sgu_b, sgu_ln_g, sgu_ln_b, f_bias):
    b, s = h.shape[:2]
    proj = h @ w_in
    cuts = [SGU_WIDTH, 2 * SGU_WIDTH, 2 * SGU_WIDTH + FOX_WIDTH,
            2 * SGU_WIDTH + 2 * FOX_WIDTH, 2 * SGU_WIDTH + 3 * FOX_WIDTH]
    u, z, q, k, v, f = jnp.split(proj, cuts, axis=-1)
    u = jax.nn.gelu(u)
    z = layernorm(jax.nn.gelu(z), sgu_ln_g, sgu_ln_b)
    z = z.reshape(b, s // SGU_CHUNK, SGU_CHUNK, SGU_GROUPS, HEAD_DIM)
    tri = jnp.tril(jnp.ones((SGU_CHUNK, SGU_CHUNK), dtype=sgu_w.dtype))
    mixed = jnp.einsum("gts,bcsgd->bctgd", sgu_w * tri, z) + jnp.transpose(sgu_b)[None, None, :, :, None]
    y_c = u * mixed.reshape(b, s, SGU_WIDTH).astype(u.dtype)
    log_f = jax.nn.log_sigmoid(f.astype(jnp.float32) + f_bias.astype(jnp.float32))
    cum = lax.cumsum(log_f, axis=1)
    q = q.reshape(b, s, FOX_HEADS, HEAD_DIM)
    k = k.reshape(b, s, FOX_HEADS, HEAD_DIM)
    v = v.reshape(b, s, FOX_HEADS, HEAD_DIM)
    y_d = fox_attention(q, k, v, cum).reshape(b, s, FOX_WIDTH).astype(y_c.dtype)
    return jnp.concatenate([y_c, y_d], axis=-1) @ w_out


def setup_inputs(seed: int = 0) -> dict:
    key = jax.random.key(seed)
    ks = jax.random.split(key, 16)

    def nrm(k, shape, scale):
        return jax.random.normal(k, shape, jnp.float32) * scale

    return {
        "x": nrm(ks[0], (BATCH, SEQ, D_MODEL), 1.0),
        "norm_g": 1.0 + nrm(ks[1], (DEPTH, 6, D_MODEL), 0.05),
        "ffn_w_in": nrm(ks[2], (DEPTH, 2, D_MODEL, 2 * D_FF), D_MODEL ** -0.5),
        "ffn_w_out": nrm(ks[3], (DEPTH, 2, D_FF, D_MODEL), D_FF ** -0.5),
        "even_w_in": nrm(ks[4], (N_EVEN, D_MODEL, EVEN_IN), D_MODEL ** -0.5),
        "even_w_out": nrm(ks[5], (N_EVEN, MIX_WIDTH, D_MODEL), MIX_WIDTH ** -0.5),
        "conv_w": nrm(ks[6], (N_EVEN, CONV_WIDTH, CONV_CH), CONV_WIDTH ** -0.5),
        "diff_lambda": nrm(ks[7], (N_EVEN, 4, HEAD_DIM), 0.1),
        "diff_subln_g": 1.0 + nrm(ks[8], (N_EVEN, DIFF_V_DIM), 0.05),
        "odd_w_in": nrm(ks[9], (N_ODD, D_MODEL, ODD_IN), D_MODEL ** -0.5),
        "odd_w_out": nrm(ks[10], (N_ODD, MIX_WIDTH, D_MODEL), MIX_WIDTH ** -0.5),
        "sgu_w": nrm(ks[11], (N_ODD, SGU_GROUPS, SGU_CHUNK, SGU_CHUNK), 0.5 * SGU_CHUNK ** -0.5),
        "sgu_b": 1.0 + nrm(ks[12], (N_ODD, SGU_GROUPS, SGU_CHUNK), 0.1),
        "sgu_ln_g": 1.0 + nrm(ks[13], (N_ODD, SGU_WIDTH), 0.05),
        "sgu_ln_b": nrm(ks[14], (N_ODD, SGU_WIDTH), 0.02),
        "fox_f_bias": jax.random.uniform(ks[15], (N_ODD, FOX_HEADS), jnp.float32, 1.0, 5.0),
    }


def reference(x, norm_g, ffn_w_in, ffn_w_out, even_w_in, even_w_out, conv_w, diff_lambda,
              diff_subln_g, odd_w_in, odd_w_out, sgu_w, sgu_b, sgu_ln_g, sgu_ln_b, fox_f_bias):
    for l in range(DEPTH):
        g = norm_g[l]
        x = x + 0.5 * rmsnorm(swiglu(rmsnorm(x, g[0]), ffn_w_in[l, 0], ffn_w_out[l, 0]), g[1])
        h = rmsnorm(x, g[2])
        i = l // 2
        if l % 2 == 0:
            m = even_mixer(h, even_w_in[i], even_w_out[i], conv_w[i], diff_lambda[i],
                           diff_subln_g[i], l)
        else:
            m = odd_mixer(h, odd_w_in[i], odd_w_out[i], sgu_w[i], sgu_b[i], sgu_ln_g[i],
                          sgu_ln_b[i], fox_f_bias[i])
        x = x + rmsnorm(m, g[3])
        x = x + 0.5 * rmsnorm(swiglu(rmsnorm(x, g[4]), ffn_w_in[l, 1], ffn_w_out[l, 1]), g[5])
    return x
```

```python
import functools
import math

import jax
import jax.numpy as jnp
import numpy as np
from jax import lax
from jax.experimental import pallas as pl
from jax.experimental.pallas import tpu as pltpu

EPS = 1e-6
F32 = jnp.float32
BF16 = jnp.bfloat16
NEG_BIG = -1e30
V7X_VMEM_LIMIT_BYTES = 60000 * 1024
LANES = 128
NORM_ROWS = 64


def _params(semantics):
    return pltpu.CompilerParams(dimension_semantics=semantics, vmem_limit_bytes=V7X_VMEM_LIMIT_BYTES)


def _pick(n, cands):
    for c in cands:
        if n % c == 0:
            return c
    raise ValueError(f"no tile in {cands} divides {n}")


def _rms_scale(v):
    return lax.rsqrt(jnp.mean(v * v, axis=-1, keepdims=True) + EPS)


def _for_row_chunks(n_rows, body):
    step = min(NORM_ROWS, n_rows)
    assert n_rows % step == 0

    def it(c, carry):
        body(pl.ds(pl.multiple_of(c * step, step), step))
        return carry

    lax.fori_loop(0, n_rows // step, it, 0)


def _gelu_tanh(v):
    return 0.5 * v * (1.0 + jnp.tanh(math.sqrt(2.0 / math.pi) * (v + 0.044715 * (v * v * v))))


def _ffn_kernel(x_ref, gpre_ref, gpost_ref, wg_ref, wu_ref, wo_ref, o_ref, h_ref):
    f = pl.program_id(1)
    tm = x_ref.shape[0]

    @pl.when(f == 0)
    def _():
        def norm_rows(rows):
            xv = x_ref[rows, :]
            h_ref[rows, :] = (xv * _rms_scale(xv) * gpre_ref[...]).astype(BF16)
            o_ref[rows, :] = jnp.zeros_like(xv)

        _for_row_chunks(tm, norm_rows)

    h = h_ref[...]
    gate = jnp.dot(h, wg_ref[...], preferred_element_type=F32)
    up = jnp.dot(h, wu_ref[...], preferred_element_type=F32)
    act = (gate / (1.0 + jnp.exp(-gate)) * up).astype(BF16)
    o_ref[...] += jnp.dot(act, wo_ref[...], preferred_element_type=F32)

    @pl.when(f == pl.num_programs(1) - 1)
    def _():
        def finish_rows(rows):
            m = o_ref[rows, :]
            o_ref[rows, :] = x_ref[rows, :] + 0.5 * (m * _rms_scale(m) * gpost_ref[...])

        _for_row_chunks(tm, finish_rows)


def _ffn(x, g_pre, g_post, w_in, w_out):
    t, d = x.shape
    d_ff = w_out.shape[0]
    tm = _pick(t, (512, 256, 128))
    tf = _pick(d_ff, (256, 128))
    nf = d_ff // tf
    return pl.pallas_call(
        _ffn_kernel,
        out_shape=jax.ShapeDtypeStruct((t, d), F32),
        grid=(t // tm, nf),
        in_specs=[
            pl.BlockSpec((tm, d), lambda i, f: (i, 0)),
            pl.BlockSpec((1, d), lambda i, f: (0, 0)),
            pl.BlockSpec((1, d), lambda i, f: (0, 0)),
            pl.BlockSpec((d, tf), lambda i, f: (0, f)),
            pl.BlockSpec((d, tf), lambda i, f: (0, f + nf)),
            pl.BlockSpec((tf, d), lambda i, f: (f, 0)),
        ],
        out_specs=pl.BlockSpec((tm, d), lambda i, f: (i, 0)),
        scratch_shapes=[pltpu.VMEM((tm, d), BF16)],
        compiler_params=_params(("parallel", "arbitrary")),
        name="ffn",
    )(x, g_pre, g_post, w_in, w_in, w_out)


def _norm_matmul_kernel(x_ref, g_ref, w_ref, o_ref, h_ref):
    @pl.when(pl.program_id(1) == 0)
    def _():
        def norm_rows(rows):
            xv = x_ref[rows, :]
            h_ref[rows, :] = (xv * _rms_scale(xv) * g_ref[...]).astype(BF16)

        _for_row_chunks(x_ref.shape[0], norm_rows)

    o_ref[...] = jnp.dot(h_ref[...], w_ref[...], preferred_element_type=F32).astype(o_ref.dtype)


def _norm_matmul(x, g, w, col0, n, out_dtype, name):
    t, d = x.shape
    tm = _pick(t, (1024, 512, 256, 128))
    tn = _pick(math.gcd(n, col0) if col0 else n, (512, 256, 128))
    off = col0 // tn
    return pl.pallas_call(
        _norm_matmul_kernel,
        out_shape=jax.ShapeDtypeStruct((t, n), out_dtype),
        grid=(t // tm, n // tn),
        in_specs=[
            pl.BlockSpec((tm, d), lambda i, j: (i, 0)),
            pl.BlockSpec((1, d), lambda i, j: (0, 0)),
            pl.BlockSpec((d, tn), lambda i, j: (0, j + off)),
        ],
        out_specs=pl.BlockSpec((tm, tn), lambda i, j: (i, j)),
        scratch_shapes=[pltpu.VMEM((tm, d), BF16)],
        compiler_params=_params(("parallel", "arbitrary")),
        name=name,
    )(x, g, w)


def _out_proj_kernel(na, ya_ref, yb_ref, w_ref, x_ref, g_ref, o_ref):
    k = pl.program_id(1)
    tm = x_ref.shape[0]

    @pl.when(k == 0)
    def _():
        o_ref[...] = jnp.dot(ya_ref[...], w_ref[...], preferred_element_type=F32)

    @pl.when(jnp.logical_and(k > 0, k < na))
    def _():
        o_ref[...] += jnp.dot(ya_ref[...], w_ref[...], preferred_element_type=F32)

    @pl.when(k >= na)
    def _():
        o_ref[...] += jnp.dot(yb_ref[...], w_ref[...], preferred_element_type=F32)

    @pl.when(k == pl.num_programs(1) - 1)
    def _():
        def finish_rows(rows):
            m = o_ref[rows, :]
            o_ref[rows, :] = x_ref[rows, :] + m * _rms_scale(m) * g_ref[...]

        _for_row_chunks(tm, finish_rows)


def _out_proj(ya, yb, w, x, g):
    t, d = x.shape
    wa, wb = ya.shape[1], yb.shape[1]
    tm = _pick(t, (512, 256, 128))
    tk = _pick(math.gcd(wa, wb), (512, 256, 128))
    na, nb = wa // tk, wb // tk
    return pl.pallas_call(
        functools.partial(_out_proj_kernel, na),
        out_shape=jax.ShapeDtypeStruct((t, d), F32),
        grid=(t // tm, na + nb),
        in_specs=[
            pl.BlockSpec((tm, tk), lambda i, k: (i, jnp.minimum(k, na - 1))),
            pl.BlockSpec((tm, tk), lambda i, k: (i, jnp.maximum(k - na, 0))),
            pl.BlockSpec((tk, d), lambda i, k: (k, 0)),
            pl.BlockSpec((tm, d), lambda i, k: (i, 0)),
            pl.BlockSpec((1, d), lambda i, k: (0, 0)),
        ],
        out_specs=pl.BlockSpec((tm, d), lambda i, k: (i, 0)),
        compiler_params=_params(("parallel", "arbitrary")),
        name="out_proj",
    )(ya, yb, w, x, g)


def _conv_kernel(gb_ref, gc_ref, xt_ref, cw_ref, o_ref, tail_ref):
    ts = gb_ref.shape[0]
    width = cw_ref.shape[0]
    hist = tail_ref.shape[0]

    @pl.when(pl.program_id(1) == 0)
    def _():
        tail_ref[...] = jnp.zeros_like(tail_ref)

    u = gc_ref[...] * xt_ref[...]
    row = lax.broadcasted_iota(jnp.int32, u.shape, 0)
    acc = cw_ref[width - 1:width, :] * u
    for back in range(1, width):
        shifted = pltpu.roll(u, back, 0)
        for r in range(back):
            prev = tail_ref[hist - back + r:hist - back + r + 1, :]
            shifted = jnp.where(row == r, prev, shifted)
        acc = acc + cw_ref[width - 1 - back:width - back, :] * shifted
    o_ref[...] = (gb_ref[...] * acc).astype(o_ref.dtype)
    tail_ref[...] = u[ts - hist:, :]


def _short_conv(pa, conv_w, batch, seq):
    t = pa.shape[0]
    c = conv_w.shape[1]
    ts = _pick(seq, (256, 128))
    tc = _pick(c, (512, 256, 128))
    nc = c // tc
    ns = seq // ts
    row = lambda b, s, j: b * ns + s
    return pl.pallas_call(
        _conv_kernel,
        out_shape=jax.ShapeDtypeStruct((t, c), BF16),
        grid=(batch * nc, ns),
        in_specs=[
            pl.BlockSpec((ts, tc), lambda bj, s: ((bj // nc) * ns + s, bj % nc)),
            pl.BlockSpec((ts, tc), lambda bj, s: ((bj // nc) * ns + s, nc + bj % nc)),
            pl.BlockSpec((ts, tc), lambda bj, s: ((bj // nc) * ns + s, 2 * nc + bj % nc)),
            pl.BlockSpec((conv_w.shape[0], tc), lambda bj, s: (0, bj % nc)),
        ],
        out_specs=pl.BlockSpec((ts, tc), lambda bj, s: ((bj // nc) * ns + s, bj % nc)),
        scratch_shapes=[pltpu.VMEM((8, tc), F32)],
        compiler_params=_params(("parallel", "arbitrary")),
        name="short_conv",
    )(pa, pa, pa, conv_w)


def _attend(q, k_ref, v_ref, kcols, vcols, bias_fn, qi, tq, scale, m_ref, l_ref, acc_ref, slot):
    m_ref[slot] = jnp.full(m_ref.shape[1:], NEG_BIG, F32)
    l_ref[slot] = jnp.zeros(l_ref.shape[1:], F32)
    acc_ref[slot] = jnp.zeros(acc_ref.shape[1:], F32)

    def chunk(j, masked):
        start = pl.multiple_of(j * tq, tq)
        k = k_ref[pl.ds(start, tq), kcols]
        v = v_ref[pl.ds(start, tq), vcols]
        s = lax.dot_general(q, k, (((1,), (1,)), ((), ())), preferred_element_type=F32)
        s = s * scale + bias_fn(start)
        if masked:
            r = lax.broadcasted_iota(jnp.int32, s.shape, 0)
            c = lax.broadcasted_iota(jnp.int32, s.shape, 1)
            s = jnp.where(r >= c, s, NEG_BIG)
        m_prev = m_ref[slot]
        m_new = jnp.maximum(m_prev, jnp.max(s, axis=-1, keepdims=True))
        alpha = jnp.exp(m_prev - m_new)
        p = jnp.exp(s - m_new)
        l_ref[slot] = alpha * l_ref[slot] + jnp.sum(p, axis=-1, keepdims=True)
        acc_ref[slot] = alpha * acc_ref[slot] + jnp.dot(p.astype(BF16), v, preferred_element_type=F32)
        m_ref[slot] = m_new

    def body(j, carry):
        chunk(j, False)
        return carry

    lax.fori_loop(0, qi, body, 0)
    chunk(qi, True)


def _diff_attn_kernel(lam_init, dh, slopes_ref, lam_ref, g_ref, q_ref, k_ref, v_ref, o_ref,
                      m_ref, l_ref, acc_ref):
    h = pl.program_id(1)
    qi = pl.program_id(2)
    tq = q_ref.shape[0]
    slope = slopes_ref[h]
    scale = dh ** -0.5
    q0 = qi * tq

    def bias_fn(start):
        kpos = lax.broadcasted_iota(jnp.int32, (1, tq), 1) + (start - q0)
        return slope * kpos.astype(F32)

    for mp in range(2):
        cols = slice(mp * dh, (mp + 1) * dh)
        _attend(q_ref[:, cols], k_ref, v_ref, cols, slice(None), bias_fn, qi, tq, scale,
                m_ref, l_ref, acc_ref, mp)

    lp = lam_ref[...]
    lam = (jnp.exp(jnp.sum(lp[0:1] * lp[1:2], axis=-1, keepdims=True))
           - jnp.exp(jnp.sum(lp[2:3] * lp[3:4], axis=-1, keepdims=True)) + lam_init)
    o = acc_ref[0] / l_ref[0] - lam * (acc_ref[1] / l_ref[1])
    o_ref[...] = (o * _rms_scale(o) * g_ref[...] * (1.0 - lam_init)).astype(o_ref.dtype)


def _alibi_slopes(n):
    def pow2(m):
        start = 2.0 ** (-8.0 / m)
        return [start ** (i + 1) for i in range(m)]

    if math.log2(n).is_integer():
        s = pow2(n)
    else:
        c = 2 ** int(math.floor(math.log2(n)))
        s = pow2(c) + pow2(2 * c)[0::2][: n - c]
    return jnp.asarray(np.array(s, dtype=np.float32))


def _diff_attention(qkv, lam_p, subln_g, lam_init, batch, seq, heads, dh):
    t = qkv.shape[0]
    dv = 2 * dh
    tq = _pick(seq, (512, 256, 128))
    nq = seq // tq
    return pl.pallas_call(
        functools.partial(_diff_attn_kernel, lam_init, dh),
        out_shape=jax.ShapeDtypeStruct((t, heads * dv), BF16),
        grid=(batch, heads, nq),
        in_specs=[
            pl.BlockSpec(memory_space=pltpu.SMEM),
            pl.BlockSpec(lam_p.shape, lambda b, h, i: (0, 0)),
            pl.BlockSpec((1, dv), lambda b, h, i: (0, 0)),
            pl.BlockSpec((tq, dv), lambda b, h, i: (b * nq + i, h)),
            pl.BlockSpec((seq, dv), lambda b, h, i: (b, heads + h)),
            pl.BlockSpec((seq, dv), lambda b, h, i: (b, 2 * heads + h)),
        ],
        out_specs=pl.BlockSpec((tq, dv), lambda b, h, i: (b * nq + i, h)),
        scratch_shapes=[
            pltpu.VMEM((2, tq, 1), F32),
            pltpu.VMEM((2, tq, 1), F32),
            pltpu.VMEM((2, tq, dv), F32),
        ],
        compiler_params=_params(("parallel", "parallel", "arbitrary")),
        name="diff_attn",
    )(_alibi_slopes(heads), lam_p, subln_g, qkv, qkv, qkv)


def _fox_attn_kernel(dh, cum_ref, q_ref, k_ref, v_ref, o_ref, m_ref, l_ref, acc_ref):
    qi = pl.program_id(2)
    tq = q_ref.shape[0]
    scale = dh ** -0.5
    q0 = pl.multiple_of(qi * tq, tq)
    for hh in range(2):
        cols = slice(hh * dh, (hh + 1) * dh)
        c0 = cum_ref[0, hh, :, pl.ds(q0, tq)][:, 0:1]

        def bias_fn(start, hh=hh, c0=c0):
            return c0 - cum_ref[0, hh, :, pl.ds(start, tq)]

        _attend(q_ref[:, cols], k_ref, v_ref, cols, cols, bias_fn, qi, tq, scale,
                m_ref, l_ref, acc_ref, hh)
    o_ref[...] = jnp.concatenate(
        [acc_ref[0] / l_ref[0], acc_ref[1] / l_ref[1]], axis=-1).astype(o_ref.dtype)


def _fox_attention(qkv, cum, batch, seq, heads, dh):
    t = qkv.shape[0]
    assert heads % 2 == 0
    pairs = heads // 2
    tq = _pick(seq, (512, 256, 128))
    nq = seq // tq
    return pl.pallas_call(
        functools.partial(_fox_attn_kernel, dh),
        out_shape=jax.ShapeDtypeStruct((t, heads * dh), BF16),
        grid=(batch, pairs, nq),
        in_specs=[
            pl.BlockSpec((1, 2, 1, seq), lambda b, h, i: (b, h, 0, 0)),
            pl.BlockSpec((tq, 2 * dh), lambda b, h, i: (b * nq + i, h)),
            pl.BlockSpec((seq, 2 * dh), lambda b, h, i: (b, pairs + h)),
            pl.BlockSpec((seq, 2 * dh), lambda b, h, i: (b, 2 * pairs + h)),
        ],
        out_specs=pl.BlockSpec((tq, 2 * dh), lambda b, h, i: (b * nq + i, h)),
        scratch_shapes=[
            pltpu.VMEM((2, tq, 1), F32),
            pltpu.VMEM((2, tq, 1), F32),
            pltpu.VMEM((2, tq, dh), F32),
        ],
        compiler_params=_params(("parallel", "parallel", "arbitrary")),
        name="fox_attn",
    )(cum, qkv, qkv, qkv)


def _forget_cumsum_kernel(f_ref, b_ref, o_ref):
    seq = f_ref.shape[0]
    ck = LANES
    r = lax.broadcasted_iota(jnp.int32, (ck, ck), 0)
    c = lax.broadcasted_iota(jnp.int32, (ck, ck), 1)
    tril = (r >= c).astype(F32)

    def body(i, carry):
        start = pl.multiple_of(i * ck, ck)
        z = f_ref[pl.ds(start, ck), :] + b_ref[...]
        log_f = jnp.minimum(z, 0.0) - jnp.log1p(jnp.exp(-jnp.abs(z)))
        cs = jnp.dot(tril, log_f, preferred_element_type=F32, precision=lax.Precision.HIGHEST) + carry
        o_ref[0, :, pl.ds(start, ck)] = cs.T
        return cs[ck - 1:ck, :]

    lax.fori_loop(0, seq // ck, body, jnp.zeros((1, ck), F32))


def _forget_cumsum(f_proj, f_bias, batch, seq):
    return pl.pallas_call(
        _forget_cumsum_kernel,
        out_shape=jax.ShapeDtypeStruct((batch, LANES, seq), F32),
        grid=(batch,),
        in_specs=[
            pl.BlockSpec((seq, LANES), lambda b: (b, 0)),
            pl.BlockSpec((1, LANES), lambda b: (0, 0)),
        ],
        out_specs=pl.BlockSpec((1, LANES, seq), lambda b: (b, 0, 0)),
        compiler_params=_params(("parallel",)),
        name="forget_cumsum",
    )(f_proj, f_bias)


def _sgu_kernel(u_ref, z_ref, w_ref, b_ref, lng_ref, lnb_ref, o_ref):
    rows, width = u_ref.shape
    groups, ck, _ = w_ref.shape
    gw = width // groups
    r = lax.broadcasted_iota(jnp.int32, (ck, ck), 0)
    c = lax.broadcasted_iota(jnp.int32, (ck, ck), 1)
    w_tri = [jnp.where(r >= c, w_ref[g], 0.0).astype(BF16) for g in range(groups)]
    for ci in range(rows // ck):
        rs = slice(ci * ck, (ci + 1) * ck)
        z = _gelu_tanh(z_ref[rs, :])
        mu = jnp.mean(z, axis=-1, keepdims=True)
        zc = z - mu
        var = jnp.mean(zc * zc, axis=-1, keepdims=True)
        zn = (zc * lax.rsqrt(var + EPS) * lng_ref[...] + lnb_ref[...]).astype(BF16)
        for g in range(groups):
            cs = slice(g * gw, (g + 1) * gw)
            mixed = jnp.dot(w_tri[g], zn[:, cs], preferred_element_type=F32) + b_ref[g]
            o_ref[rs, cs] = (_gelu_tanh(u_ref[rs, cs]) * mixed).astype(o_ref.dtype)


def _sgu(uz, sgu_w, sgu_b, ln_g, ln_b):
    t = uz.shape[0]
    groups, ck, _ = sgu_w.shape
    width = ln_g.shape[1]
    rows = _pick(t, (2 * ck, ck))
    return pl.pallas_call(
        _sgu_kernel,
        out_shape=jax.ShapeDtypeStruct((t, width), BF16),
        grid=(t // rows,),
        in_specs=[
            pl.BlockSpec((rows, width), lambda i: (i, 0)),
            pl.BlockSpec((rows, width), lambda i: (i, 1)),
            pl.BlockSpec(sgu_w.shape, lambda i: (0, 0, 0)),
            pl.BlockSpec(sgu_b.shape, lambda i: (0, 0, 0)),
            pl.BlockSpec((1, width), lambda i: (0, 0)),
            pl.BlockSpec((1, width), lambda i: (0, 0)),
        ],
        out_specs=pl.BlockSpec((rows, width), lambda i: (i, 0)),
        compiler_params=_params(("parallel",)),
        name="sgu",
    )(uz, uz, sgu_w, sgu_b, ln_g, ln_b)


def _even_mixer(x, g_pre, g_post, w_in, w_out, conv_w, lam_p, subln_g, layer_idx, batch, seq):
    d = x.shape[1]
    conv_ch = conv_w.shape[1]
    dh = lam_p.shape[1]
    dv = subln_g.shape[0]
    heads = (d - conv_ch) // dv
    w_in = w_in.astype(BF16)
    pa = _norm_matmul(x, g_pre, w_in, 0, 3 * conv_ch, F32, "even_in_conv")
    qkv = _norm_matmul(x, g_pre, w_in, 3 * conv_ch, w_in.shape[1] - 3 * conv_ch, BF16, "even_in_attn")
    ya = _short_conv(pa, conv_w, batch, seq)
    lam_init = 0.8 - 0.6 * math.exp(-0.3 * layer_idx)
    yb = _diff_attention(qkv, lam_p, subln_g[None, :], lam_init, batch, seq, heads, dh)
    return _out_proj(ya, yb, w_out.astype(BF16), x, g_post)


def _odd_mixer(x, g_pre, g_post, w_in, w_out, sgu_w, sgu_b, ln_g, ln_b, f_bias, batch, seq):
    d = x.shape[1]
    width = ln_g.shape[0]
    heads = f_bias.shape[0]
    dh = (d - width) // heads
    n_attn = 3 * heads * dh
    w_bf = w_in.astype(BF16)
    uz = _norm_matmul(x, g_pre, w_bf, 0, 2 * width, F32, "odd_in_sgu")
    qkv = _norm_matmul(x, g_pre, w_bf, 2 * width, n_attn, BF16, "odd_in_attn")
    w_f = jnp.pad(w_bf[:, 2 * width + n_attn:], ((0, 0), (0, LANES - heads)))
    f_proj = _norm_matmul(x, g_pre, w_f, 0, LANES, F32, "odd_in_gate")
    bias = jnp.pad(f_bias, (0, LANES - heads))[None, :]
    cum = _forget_cumsum(f_proj, bias, batch, seq)[:, :heads].reshape(batch, heads, 1, seq)
    yc = _sgu(uz, sgu_w, sgu_b[:, :, None], ln_g[None, :], ln_b[None, :])
    yd = _fox_attention(qkv, cum, batch, seq, heads, dh)
    return _out_proj(yc, yd, w_out.astype(BF16), x, g_post)


def kernel(x, norm_g, ffn_w_in, ffn_w_out, even_w_in, even_w_out, conv_w, diff_lambda, diff_subln_g,
           odd_w_in, odd_w_out, sgu_w, sgu_b, sgu_ln_g, sgu_ln_b, fox_f_bias):
    batch, seq, d = x.shape
    depth = norm_g.shape[0]
    xt = x.reshape(batch * seq, d)
    for l in range(depth):
        g = norm_g[l][:, None, :]
        xt = _ffn(xt, g[0], g[1], ffn_w_in[l, 0].astype(BF16), ffn_w_out[l, 0].astype(BF16))
        i = l // 2
        if l % 2 == 0:
            xt = _even_mixer(xt, g[2], g[3], even_w_in[i], even_w_out[i], conv_w[i], diff_lambda[i],
                             diff_subln_g[i], l, batch, seq)
        else:
            xt = _odd_mixer(xt, g[2], g[3], odd_w_in[i], odd_w_out[i], sgu_w[i], sgu_b[i],
                            sgu_ln_g[i], sgu_ln_b[i], fox_f_bias[i], batch, seq)
        xt = _ffn(xt, g[4], g[5], ffn_w_in[l, 1].astype(BF16), ffn_w_out[l, 1].astype(BF16))
    return xt.reshape(batch, seq, d)
```

```python
import functools
import math

import jax
import jax.numpy as jnp
import numpy as np
from jax import lax
from jax.experimental import pallas as pl
from jax.experimental.pallas import tpu as pltpu

EPS = 1e-6
F32 = jnp.float32
BF16 = jnp.bfloat16
NEG_BIG = -1e30
V7X_VMEM_LIMIT_BYTES = 60000 * 1024
LANES = 128
LOG2E = math.log2(math.e)
NORM_ROWS = 128


def _params(semantics):
    return pltpu.CompilerParams(dimension_semantics=semantics, vmem_limit_bytes=V7X_VMEM_LIMIT_BYTES)


def _pick(n, cands):
    for c in cands:
        if n % c == 0:
            return c
    raise ValueError(f"no tile in {cands} divides {n}")


def _rms_scale(v):
    return lax.rsqrt(jnp.mean(v * v, axis=-1, keepdims=True) + EPS)


def _for_row_chunks(n_rows, body):
    step = min(NORM_ROWS, n_rows)
    assert n_rows % step == 0

    def it(c, carry):
        body(pl.ds(pl.multiple_of(c * step, step), step))
        return carry

    lax.fori_loop(0, n_rows // step, it, 0)


def _gelu_tanh(v):
    return 0.5 * v * (1.0 + jnp.tanh(math.sqrt(2.0 / math.pi) * (v + 0.044715 * (v * v * v))))


def _ffn_kernel(x_ref, gpre_ref, gpost_ref, wg_ref, wu_ref, wo_ref, o_ref, h_ref):
    f = pl.program_id(1)
    tm = x_ref.shape[0]

    @pl.when(f == 0)
    def _():
        def norm_rows(rows):
            xv = x_ref[rows, :]
            h_ref[rows, :] = (xv * _rms_scale(xv) * gpre_ref[...]).astype(BF16)
            o_ref[rows, :] = jnp.zeros_like(xv)

        _for_row_chunks(tm, norm_rows)

    h = h_ref[...]
    gate = jnp.dot(h, wg_ref[...], preferred_element_type=F32)
    up = jnp.dot(h, wu_ref[...], preferred_element_type=F32)
    act = (gate / (1.0 + jnp.exp(-gate)) * up).astype(BF16)
    o_ref[...] += jnp.dot(act, wo_ref[...], preferred_element_type=F32)

    @pl.when(f == pl.num_programs(1) - 1)
    def _():
        def finish_rows(rows):
            m = o_ref[rows, :]
            o_ref[rows, :] = x_ref[rows, :] + 0.5 * (m * _rms_scale(m) * gpost_ref[...])

        _for_row_chunks(tm, finish_rows)


def _ffn(x, g_pre, g_post, w_in, w_out):
    t, d = x.shape
    d_ff = w_out.shape[0]
    tm = _pick(t, (512, 256, 128))
    tf = _pick(d_ff, (256, 128))
    nf = d_ff // tf
    return pl.pallas_call(
        _ffn_kernel,
        out_shape=jax.ShapeDtypeStruct((t, d), F32),
        grid=(t // tm, nf),
        in_specs=[
            pl.BlockSpec((tm, d), lambda i, f: (i, 0)),
            pl.BlockSpec((1, d), lambda i, f: (0, 0)),
            pl.BlockSpec((1, d), lambda i, f: (0, 0)),
            pl.BlockSpec((d, tf), lambda i, f: (0, f)),
            pl.BlockSpec((d, tf), lambda i, f: (0, f + nf)),
            pl.BlockSpec((tf, d), lambda i, f: (f, 0)),
        ],
        out_specs=pl.BlockSpec((tm, d), lambda i, f: (i, 0)),
        scratch_shapes=[pltpu.VMEM((tm, d), BF16)],
        compiler_params=_params(("parallel", "arbitrary")),
        name="ffn",
    )(x, g_pre, g_post, w_in, w_in, w_out)


def _norm_matmul_kernel(x_ref, g_ref, w_ref, cs_ref, o_ref, h_ref):
    @pl.when(pl.program_id(1) == 0)
    def _():
        def norm_rows(rows):
            xv = x_ref[rows, :]
            h_ref[rows, :] = (xv * _rms_scale(xv) * g_ref[...]).astype(BF16)

        _for_row_chunks(x_ref.shape[0], norm_rows)

    acc = jnp.dot(h_ref[...], w_ref[...], preferred_element_type=F32)
    o_ref[...] = (acc * cs_ref[...]).astype(o_ref.dtype)


def _norm_matmul(x, g, w, col0, n, out_dtype, name, col_scale=None):
    t, d = x.shape
    tm = _pick(t, (1024, 512, 256, 128))
    tn = _pick(math.gcd(n, col0) if col0 else n, (512, 256, 128))
    off = col0 // tn
    if col_scale is None:
        col_scale = jnp.ones((1, n), F32)
    return pl.pallas_call(
        _norm_matmul_kernel,
        out_shape=jax.ShapeDtypeStruct((t, n), out_dtype),
        grid=(t // tm, n // tn),
        in_specs=[
            pl.BlockSpec((tm, d), lambda i, j: (i, 0)),
            pl.BlockSpec((1, d), lambda i, j: (0, 0)),
            pl.BlockSpec((d, tn), lambda i, j: (0, j + off)),
            pl.BlockSpec((1, tn), lambda i, j: (0, j)),
        ],
        out_specs=pl.BlockSpec((tm, tn), lambda i, j: (i, j)),
        scratch_shapes=[pltpu.VMEM((tm, d), BF16)],
        compiler_params=_params(("parallel", "arbitrary")),
        name=name,
    )(x, g, w, col_scale)


def _out_proj_kernel(na, ya_ref, yb_ref, w_ref, x_ref, g_ref, o_ref):
    k = pl.program_id(1)
    tm = x_ref.shape[0]

    @pl.when(k == 0)
    def _():
        o_ref[...] = jnp.dot(ya_ref[...], w_ref[...], preferred_element_type=F32)

    @pl.when(jnp.logical_and(k > 0, k < na))
    def _():
        o_ref[...] += jnp.dot(ya_ref[...], w_ref[...], preferred_element_type=F32)

    @pl.when(k >= na)
    def _():
        o_ref[...] += jnp.dot(yb_ref[...], w_ref[...], preferred_element_type=F32)

    @pl.when(k == pl.num_programs(1) - 1)
    def _():
        def finish_rows(rows):
            m = o_ref[rows, :]
            o_ref[rows, :] = x_ref[rows, :] + m * _rms_scale(m) * g_ref[...]

        _for_row_chunks(tm, finish_rows)


def _out_proj(ya, yb, w, x, g):
    t, d = x.shape
    wa, wb = ya.shape[1], yb.shape[1]
    tm = _pick(t, (512, 256, 128))
    tk = _pick(math.gcd(wa, wb), (1024, 512, 256, 128))
    na, nb = wa // tk, wb // tk
    return pl.pallas_call(
        functools.partial(_out_proj_kernel, na),
        out_shape=jax.ShapeDtypeStruct((t, d), F32),
        grid=(t // tm, na + nb),
        in_specs=[
            pl.BlockSpec((tm, tk), lambda i, k: (i, jnp.minimum(k, na - 1))),
            pl.BlockSpec((tm, tk), lambda i, k: (i, jnp.maximum(k - na, 0))),
            pl.BlockSpec((tk, d), lambda i, k: (k, 0)),
            pl.BlockSpec((tm, d), lambda i, k: (i, 0)),
            pl.BlockSpec((1, d), lambda i, k: (0, 0)),
        ],
        out_specs=pl.BlockSpec((tm, d), lambda i, k: (i, 0)),
        compiler_params=_params(("parallel", "arbitrary")),
        name="out_proj",
    )(ya, yb, w, x, g)


def _conv_kernel(gb_ref, gc_ref, xt_ref, cw_ref, o_ref, tail_ref):
    ts = gb_ref.shape[0]
    width = cw_ref.shape[0]
    hist = tail_ref.shape[0]

    @pl.when(pl.program_id(1) == 0)
    def _():
        tail_ref[...] = jnp.zeros_like(tail_ref)

    u = gc_ref[...] * xt_ref[...]
    row = lax.broadcasted_iota(jnp.int32, u.shape, 0)
    acc = cw_ref[width - 1:width, :] * u
    for back in range(1, width):
        shifted = pltpu.roll(u, back, 0)
        for r in range(back):
            prev = tail_ref[hist - back + r:hist - back + r + 1, :]
            shifted = jnp.where(row == r, prev, shifted)
        acc = acc + cw_ref[width - 1 - back:width - back, :] * shifted
    o_ref[...] = (gb_ref[...] * acc).astype(o_ref.dtype)
    tail_ref[...] = u[ts - hist:, :]


def _short_conv(pa, conv_w, batch, seq):
    t = pa.shape[0]
    c = conv_w.shape[1]
    ts = _pick(seq, (256, 128))
    tc = _pick(c, (512, 256, 128))
    nc = c // tc
    ns = seq // ts
    return pl.pallas_call(
        _conv_kernel,
        out_shape=jax.ShapeDtypeStruct((t, c), BF16),
        grid=(batch * nc, ns),
        in_specs=[
            pl.BlockSpec((ts, tc), lambda bj, s: ((bj // nc) * ns + s, bj % nc)),
            pl.BlockSpec((ts, tc), lambda bj, s: ((bj // nc) * ns + s, nc + bj % nc)),
            pl.BlockSpec((ts, tc), lambda bj, s: ((bj // nc) * ns + s, 2 * nc + bj % nc)),
            pl.BlockSpec((conv_w.shape[0], tc), lambda bj, s: (0, bj % nc)),
        ],
        out_specs=pl.BlockSpec((ts, tc), lambda bj, s: ((bj // nc) * ns + s, bj % nc)),
        scratch_shapes=[pltpu.VMEM((8, tc), F32)],
        compiler_params=_params(("parallel", "arbitrary")),
        name="short_conv",
    )(pa, pa, pa, conv_w)


def _transposed_bf16(block):
    return block.astype(F32).T.astype(BF16)


def _init_softmax_state(m_ref, l_ref, acc_ref, slot):
    m_ref[slot] = jnp.full(m_ref.shape[1:], NEG_BIG, F32)
    l_ref[slot] = jnp.zeros(l_ref.shape[1:], F32)
    acc_ref[slot] = jnp.zeros(acc_ref.shape[1:], F32)


def _softmax_chunk(s, off, masked, v_t, m_ref, l_ref, acc_ref, slot):
    if masked:
        key = lax.broadcasted_iota(jnp.int32, s.shape, 0)
        qry = lax.broadcasted_iota(jnp.int32, s.shape, 1)
        s = jnp.where(qry >= key, s, NEG_BIG)
    m_prev = m_ref[slot]
    m_new = jnp.maximum(m_prev, jnp.max(s, axis=0, keepdims=True) + off)
    alpha = jnp.exp2(m_prev - m_new)
    p = jnp.exp2(s - (m_new - off))
    l_ref[slot] = alpha * l_ref[slot] + jnp.sum(p, axis=0, keepdims=True)
    acc_ref[slot] = alpha * acc_ref[slot] + jnp.dot(v_t, p.astype(BF16), preferred_element_type=F32)
    m_ref[slot] = m_new


def _causal_chunks(qi, tq, chunk):
    def body(j, carry):
        chunk(pl.multiple_of(j * (2 * tq), 2 * tq), 2 * tq, False)
        return carry

    lax.fori_loop(0, qi // 2, body, 0)

    @pl.when(qi % 2 == 1)
    def _():
        chunk(pl.multiple_of((qi - 1) * tq, 2 * tq), tq, False)

    chunk(pl.multiple_of(qi * tq, tq), tq, True)


def _diff_attn_kernel(lam_init, dh, slopes_ref, lam_ref, g_ref, q_ref, k_ref, v_ref, o_ref,
                      qt_ref, vt_ref, bias_ref, m_ref, l_ref, acc_ref):
    h = pl.program_id(1)
    qi = pl.program_id(2)
    tq = q_ref.shape[0]
    seq = k_ref.shape[0]

    slope2 = slopes_ref[h] * LOG2E

    @pl.when(qi == 0)
    def _():
        for c in range(seq // tq):
            rows = slice(c * tq, (c + 1) * tq)
            vt_ref[:, rows] = _transposed_bf16(v_ref[rows, :])
        bias_ref[...] = lax.broadcasted_iota(jnp.int32, bias_ref.shape, 0).astype(F32) * slope2

    for mp in range(2):
        qt_ref[mp] = _transposed_bf16(q_ref[:, mp * dh:(mp + 1) * dh])
        _init_softmax_state(m_ref, l_ref, acc_ref, mp)

    def chunk(start, n, masked):
        off = ((start - qi * tq) + jnp.zeros((1, tq), jnp.int32)).astype(F32) * slope2
        v_t = vt_ref[:, pl.ds(start, n)]
        for mp in range(2):
            k = k_ref[pl.ds(start, n), mp * dh:(mp + 1) * dh]
            s = jnp.dot(k, qt_ref[mp], preferred_element_type=F32) + bias_ref[0:n, :]
            _softmax_chunk(s, off, masked, v_t, m_ref, l_ref, acc_ref, mp)

    _causal_chunks(qi, tq, chunk)

    lp = lam_ref[...]
    lam = (jnp.exp(jnp.sum(lp[0:1] * lp[1:2], axis=-1, keepdims=True))
           - jnp.exp(jnp.sum(lp[2:3] * lp[3:4], axis=-1, keepdims=True)) + lam_init)
    o_t = acc_ref[0] * (1.0 / l_ref[0]) - lam * (acc_ref[1] * (1.0 / l_ref[1]))
    rms = lax.rsqrt(jnp.mean(o_t * o_t, axis=0, keepdims=True) + EPS)
    o_ref[...] = ((o_t * rms).T * g_ref[...] * (1.0 - lam_init)).astype(o_ref.dtype)


def _alibi_slopes(n):
    def pow2(m):
        start = 2.0 ** (-8.0 / m)
        return [start ** (i + 1) for i in range(m)]

    if math.log2(n).is_integer():
        s = pow2(n)
    else:
        c = 2 ** int(math.floor(math.log2(n)))
        s = pow2(c) + pow2(2 * c)[0::2][: n - c]
    return jnp.asarray(np.array(s, dtype=np.float32))


def _diff_attention(qkv, lam_p, subln_g, lam_init, batch, seq, heads, dh):
    t = qkv.shape[0]
    dv = 2 * dh
    tq = _pick(seq, (512, 256, 128))
    nq = seq // tq
    return pl.pallas_call(
        functools.partial(_diff_attn_kernel, lam_init, dh),
        out_shape=jax.ShapeDtypeStruct((t, heads * dv), BF16),
        grid=(batch, heads, nq),
        in_specs=[
            pl.BlockSpec(memory_space=pltpu.SMEM),
            pl.BlockSpec(lam_p.shape, lambda b, h, i: (0, 0)),
            pl.BlockSpec((1, dv), lambda b, h, i: (0, 0)),
            pl.BlockSpec((tq, dv), lambda b, h, i: (b * nq + i, h)),
            pl.BlockSpec((seq, dv), lambda b, h, i: (b, heads + h)),
            pl.BlockSpec((seq, dv), lambda b, h, i: (b, 2 * heads + h)),
        ],
        out_specs=pl.BlockSpec((tq, dv), lambda b, h, i: (b * nq + i, h)),
        scratch_shapes=[
            pltpu.VMEM((2, dh, tq), BF16),
            pltpu.VMEM((dv, seq), BF16),
            pltpu.VMEM((2 * tq, tq), F32),
            pltpu.VMEM((2, 1, tq), F32),
            pltpu.VMEM((2, 1, tq), F32),
            pltpu.VMEM((2, dv, tq), F32),
        ],
        compiler_params=_params(("parallel", "parallel", "arbitrary")),
        name="diff_attn",
    )(_alibi_slopes(heads), lam_p, subln_g, qkv, qkv, qkv)


def _fox_attn_kernel(dh, cum_ref, q_ref, k_ref, v_ref, o_ref,
                     qt_ref, vt_ref, rel_ref, first_ref, m_ref, l_ref, acc_ref):
    pair = pl.program_id(1)
    qi = pl.program_id(2)
    tq = q_ref.shape[0]
    seq = k_ref.shape[0]
    reps = tq // LANES

    span = 2 * tq

    @pl.when(qi == 0)
    def _():
        lane = lax.broadcasted_iota(jnp.int32, (span, LANES), 1)
        for c in range(seq // span):
            rows = slice(c * span, (c + 1) * span)
            cum = cum_ref[rows, :]
            for hh in range(2):
                vt_ref[hh, :, rows] = _transposed_bf16(v_ref[rows, hh * dh:(hh + 1) * dh])
                col = jnp.sum(jnp.where(lane == 2 * pair + hh, cum, 0.0), axis=1, keepdims=True)
                first = col[0:1, :]
                rel_ref[hh, rows, :] = jnp.broadcast_to((first - col) * LOG2E, (span, LANES))
                first_ref[hh, c:c + 1, :] = jnp.broadcast_to(first, (1, LANES))

    for hh in range(2):
        qt_ref[hh] = _transposed_bf16(q_ref[:, hh * dh:(hh + 1) * dh])
        _init_softmax_state(m_ref, l_ref, acc_ref, hh)

    def chunk(start, n, masked):
        for hh in range(2):
            k = k_ref[pl.ds(start, n), hh * dh:(hh + 1) * dh]
            rel = rel_ref[hh, pl.ds(start, n), :]
            s = jnp.dot(k, qt_ref[hh], preferred_element_type=F32) + jnp.concatenate([rel] * reps, axis=1)
            off = (first_ref[hh, pl.ds(qi // 2, 1), :] - first_ref[hh, pl.ds(start // span, 1), :]) * LOG2E
            off = jnp.concatenate([off] * reps, axis=1)
            _softmax_chunk(s, off, masked, vt_ref[hh, :, pl.ds(start, n)], m_ref, l_ref, acc_ref, hh)

    _causal_chunks(qi, tq, chunk)

    for hh in range(2):
        o_t = acc_ref[hh] * (1.0 / l_ref[hh])
        o_ref[:, hh * dh:(hh + 1) * dh] = o_t.T.astype(o_ref.dtype)


def _fox_attention(qkv, cum, batch, seq, heads, dh):
    t = qkv.shape[0]
    assert heads % 2 == 0
    pairs = heads // 2
    tq = _pick(seq, (512, 256, 128))
    nq = seq // tq
    assert nq % 2 == 0
    return pl.pallas_call(
        functools.partial(_fox_attn_kernel, dh),
        out_shape=jax.ShapeDtypeStruct((t, heads * dh), BF16),
        grid=(batch, pairs, nq),
        in_specs=[
            pl.BlockSpec((seq, LANES), lambda b, h, i: (b, 0)),
            pl.BlockSpec((tq, 2 * dh), lambda b, h, i: (b * nq + i, h)),
            pl.BlockSpec((seq, 2 * dh), lambda b, h, i: (b, pairs + h)),
            pl.BlockSpec((seq, 2 * dh), lambda b, h, i: (b, 2 * pairs + h)),
        ],
        out_specs=pl.BlockSpec((tq, 2 * dh), lambda b, h, i: (b * nq + i, h)),
        scratch_shapes=[
            pltpu.VMEM((2, dh, tq), BF16),
            pltpu.VMEM((2, dh, seq), BF16),
            pltpu.VMEM((2, seq, LANES), F32),
            pltpu.VMEM((2, nq, LANES), F32),
            pltpu.VMEM((2, 1, tq), F32),
            pltpu.VMEM((2, 1, tq), F32),
            pltpu.VMEM((2, dh, tq), F32),
        ],
        compiler_params=_params(("parallel", "parallel", "arbitrary")),
        name="fox_attn",
    )(cum, qkv, qkv, qkv)


def _forget_cumsum_kernel(f_ref, b_ref, o_ref):
    seq = f_ref.shape[0]
    ck = LANES
    r = lax.broadcasted_iota(jnp.int32, (ck, ck), 0)
    c = lax.broadcasted_iota(jnp.int32, (ck, ck), 1)
    tril = (r >= c).astype(F32)

    def body(i, carry):
        rows = pl.ds(pl.multiple_of(i * ck, ck), ck)
        z = f_ref[rows, :] + b_ref[...]
        log_f = jnp.minimum(z, 0.0) - jnp.log1p(jnp.exp(-jnp.abs(z)))
        cs = jnp.dot(tril, log_f, preferred_element_type=F32, precision=lax.Precision.HIGHEST) + carry
        o_ref[rows, :] = cs
        return cs[ck - 1:ck, :]

    lax.fori_loop(0, seq // ck, body, jnp.zeros((1, ck), F32))


def _forget_cumsum(f_proj, f_bias, batch, seq):
    return pl.pallas_call(
        _forget_cumsum_kernel,
        out_shape=jax.ShapeDtypeStruct(f_proj.shape, F32),
        grid=(batch,),
        in_specs=[
            pl.BlockSpec((seq, LANES), lambda b: (b, 0)),
            pl.BlockSpec((1, LANES), lambda b: (0, 0)),
        ],
        out_specs=pl.BlockSpec((seq, LANES), lambda b: (b, 0)),
        compiler_params=_params(("parallel",)),
        name="forget_cumsum",
    )(f_proj, f_bias)


def _sgu_kernel(u_ref, z_ref, w_ref, b_ref, lng_ref, lnb_ref, o_ref):
    rows, width = u_ref.shape
    groups, ck, _ = w_ref.shape
    gw = width // groups
    r = lax.broadcasted_iota(jnp.int32, (ck, ck), 0)
    c = lax.broadcasted_iota(jnp.int32, (ck, ck), 1)
    w_tri = [jnp.where(r >= c, w_ref[g], 0.0).astype(BF16) for g in range(groups)]
    for ci in range(rows // ck):
        rs = slice(ci * ck, (ci + 1) * ck)
        z = _gelu_tanh(z_ref[rs, :])
        mu = jnp.mean(z, axis=-1, keepdims=True)
        zc = z - mu
        var = jnp.mean(zc * zc, axis=-1, keepdims=True)
        zn = (zc * lax.rsqrt(var + EPS) * lng_ref[...] + lnb_ref[...]).astype(BF16)
        for g in range(groups):
            cs = slice(g * gw, (g + 1) * gw)
            mixed = jnp.dot(w_tri[g], zn[:, cs], preferred_element_type=F32) + b_ref[g]
            o_ref[rs, cs] = (_gelu_tanh(u_ref[rs, cs]) * mixed).astype(o_ref.dtype)


def _sgu(uz, sgu_w, sgu_b, ln_g, ln_b):
    t = uz.shape[0]
    groups, ck, _ = sgu_w.shape
    width = ln_g.shape[1]
    rows = _pick(t, (2 * ck, ck))
    return pl.pallas_call(
        _sgu_kernel,
        out_shape=jax.ShapeDtypeStruct((t, width), BF16),
        grid=(t // rows,),
        in_specs=[
            pl.BlockSpec((rows, width), lambda i: (i, 0)),
            pl.BlockSpec((rows, width), lambda i: (i, 1)),
            pl.BlockSpec(sgu_w.shape, lambda i: (0, 0, 0)),
            pl.BlockSpec(sgu_b.shape, lambda i: (0, 0, 0)),
            pl.BlockSpec((1, width), lambda i: (0, 0)),
            pl.BlockSpec((1, width), lambda i: (0, 0)),
        ],
        out_specs=pl.BlockSpec((rows, width), lambda i: (i, 0)),
        compiler_params=_params(("parallel",)),
        name="sgu",
    )(uz, uz, sgu_w, sgu_b, ln_g, ln_b)


def _query_col_scale(n, n_query, dh):
    return jnp.where(jnp.arange(n) < n_query, dh ** -0.5 * LOG2E, 1.0).astype(F32)[None, :]


def _even_mixer(x, g_pre, g_post, w_in, w_out, conv_w, lam_p, subln_g, layer_idx, batch, seq):
    d = x.shape[1]
    conv_ch = conv_w.shape[1]
    dh = lam_p.shape[1]
    dv = subln_g.shape[0]
    heads = (d - conv_ch) // dv
    w_in = w_in.astype(BF16)
    n_attn = w_in.shape[1] - 3 * conv_ch
    pa = _norm_matmul(x, g_pre, w_in, 0, 3 * conv_ch, F32, "even_in_conv")
    qkv = _norm_matmul(x, g_pre, w_in, 3 * conv_ch, n_attn, BF16, "even_in_attn",
                       _query_col_scale(n_attn, 2 * heads * dh, dh))
    ya = _short_conv(pa, conv_w, batch, seq)
    lam_init = 0.8 - 0.6 * math.exp(-0.3 * layer_idx)
    yb = _diff_attention(qkv, lam_p, subln_g[None, :], lam_init, batch, seq, heads, dh)
    return _out_proj(ya, yb, w_out.astype(BF16), x, g_post)


def _odd_mixer(x, g_pre, g_post, w_in, w_out, sgu_w, sgu_b, ln_g, ln_b, f_bias, batch, seq):
    d = x.shape[1]
    width = ln_g.shape[0]
    heads = f_bias.shape[0]
    dh = (d - width) // heads
    n_attn = 3 * heads * dh
    w_bf = w_in.astype(BF16)
    uz = _norm_matmul(x, g_pre, w_bf, 0, 2 * width, F32, "odd_in_sgu")
    qkv = _norm_matmul(x, g_pre, w_bf, 2 * width, n_attn, BF16, "odd_in_attn",
                       _query_col_scale(n_attn, heads * dh, dh))
    w_f = jnp.pad(w_bf[:, 2 * width + n_attn:], ((0, 0), (0, LANES - heads)))
    f_proj = _norm_matmul(x, g_pre, w_f, 0, LANES, F32, "odd_in_gate")
    bias = jnp.pad(f_bias, (0, LANES - heads))[None, :]
    cum = _forget_cumsum(f_proj, bias, batch, seq)
    yc = _sgu(uz, sgu_w, sgu_b[:, :, None], ln_g[None, :], ln_b[None, :])
    yd = _fox_attention(qkv, cum, batch, seq, heads, dh)
    return _out_proj(yc, yd, w_out.astype(BF16), x, g_post)


def kernel(x, norm_g, ffn_w_in, ffn_w_out, even_w_in, even_w_out, conv_w, diff_lambda, diff_subln_g,
           odd_w_in, odd_w_out, sgu_w, sgu_b, sgu_ln_g, sgu_ln_b, fox_f_bias):
    batch, seq, d = x.shape
    depth = norm_g.shape[0]
    xt = x.reshape(batch * seq, d)
    for l in range(depth):
        g = norm_g[l][:, None, :]
        xt = _ffn(xt, g[0], g[1], ffn_w_in[l, 0].astype(BF16), ffn_w_out[l, 0].astype(BF16))
        i = l // 2
        if l % 2 == 0:
            xt = _even_mixer(xt, g[2], g[3], even_w_in[i], even_w_out[i], conv_w[i], diff_lambda[i],
                             diff_subln_g[i], l, batch, seq)
        else:
            xt = _odd_mixer(xt, g[2], g[3], odd_w_in[i], odd_w_out[i], sgu_w[i], sgu_b[i],
                            sgu_ln_g[i], sgu_ln_b[i], fox_f_bias[i], batch, seq)
        xt = _ffn(xt, g[4], g[5], ffn_w_in[l, 1].astype(BF16), ffn_w_out[l, 1].astype(BF16))
    return xt.reshape(batch, seq, d)
```

```python
import functools
import math

import jax
import jax.numpy as jnp
import numpy as np
from jax import lax
from jax.experimental import pallas as pl
from jax.experimental.pallas import tpu as pltpu

EPS = 1e-6
F32 = jnp.float32
BF16 = jnp.bfloat16
NEG_BIG = -1e30
V7X_VMEM_LIMIT_BYTES = 60000 * 1024
LANES = 128
LOG2E = math.log2(math.e)
NORM_ROWS = 128
NORM_SUB_ROWS = 16


def _params(semantics):
    return pltpu.CompilerParams(dimension_semantics=semantics, vmem_limit_bytes=V7X_VMEM_LIMIT_BYTES)


def _pick(n, cands):
    for c in cands:
        if n % c == 0:
            return c
    raise ValueError(f"no tile in {cands} divides {n}")


def _rms_scale(v):
    return lax.rsqrt(jnp.mean(v * v, axis=-1, keepdims=True) + EPS)


def _normalise_rows(n_rows, src_ref, apply):
    step = min(NORM_ROWS, n_rows)
    sub = min(NORM_SUB_ROWS, step)
    assert n_rows % step == 0 and step % sub == 0

    def it(c, carry):
        first = pl.multiple_of(c * step, step)
        scale = _rms_scale(src_ref[pl.ds(first, step), :])
        for k in range(step // sub):
            apply(pl.ds(first + k * sub, sub), scale[k * sub:(k + 1) * sub])
        return carry

    lax.fori_loop(0, n_rows // step, it, 0)


def _gelu_tanh(v):
    return 0.5 * v * (1.0 + jnp.tanh(math.sqrt(2.0 / math.pi) * (v + 0.044715 * (v * v * v))))


def _ffn_kernel(x_ref, gpre_ref, gpost_ref, wg_ref, wu_ref, wo_ref, o_ref, h_ref):
    f = pl.program_id(1)
    tm = x_ref.shape[0]

    @pl.when(f == 0)
    def _():
        def norm_rows(rows, scale):
            h_ref[rows, :] = (x_ref[rows, :] * scale * gpre_ref[...]).astype(BF16)
            o_ref[rows, :] = jnp.zeros((rows.size, x_ref.shape[1]), F32)

        _normalise_rows(tm, x_ref, norm_rows)

    h = h_ref[...]
    gate = jnp.dot(h, wg_ref[...], preferred_element_type=F32)
    up = jnp.dot(h, wu_ref[...], preferred_element_type=F32)
    act = (gate / (1.0 + jnp.exp(-gate)) * up).astype(BF16)
    o_ref[...] += jnp.dot(act, wo_ref[...], preferred_element_type=F32)

    @pl.when(f == pl.num_programs(1) - 1)
    def _():
        def finish_rows(rows, scale):
            o_ref[rows, :] = x_ref[rows, :] + o_ref[rows, :] * (0.5 * scale) * gpost_ref[...]

        _normalise_rows(tm, o_ref, finish_rows)


def _ffn(x, g_pre, g_post, w_in, w_out, layer, half):
    t, d = x.shape
    d_ff = w_out.shape[2]
    tm = _pick(t, (512, 256, 128))
    tf = _pick(d_ff, (256, 128))
    nf = d_ff // tf
    return pl.pallas_call(
        _ffn_kernel,
        out_shape=jax.ShapeDtypeStruct((t, d), F32),
        grid=(t // tm, nf),
        in_specs=[
            pl.BlockSpec((tm, d), lambda i, f: (i, 0)),
            pl.BlockSpec((1, d), lambda i, f: (0, 0)),
            pl.BlockSpec((1, d), lambda i, f: (0, 0)),
            pl.BlockSpec((None, None, d, tf), lambda i, f: (layer, half, 0, f)),
            pl.BlockSpec((None, None, d, tf), lambda i, f: (layer, half, 0, f + nf)),
            pl.BlockSpec((None, None, tf, d), lambda i, f: (layer, half, f, 0)),
        ],
        out_specs=pl.BlockSpec((tm, d), lambda i, f: (i, 0)),
        scratch_shapes=[pltpu.VMEM((tm, d), BF16)],
        compiler_params=_params(("parallel", "arbitrary")),
        name="ffn",
    )(x, g_pre, g_post, w_in, w_in, w_out)


def _in_proj_kernel(seg_steps, scaled, x_ref, g_ref, w_ref, cs_ref, *refs):
    out_refs, h_ref = refs[:-1], refs[-1]
    j = pl.program_id(1)

    @pl.when(j == 0)
    def _():
        def norm_rows(rows, scale):
            h_ref[rows, :] = (x_ref[rows, :] * scale * g_ref[...]).astype(BF16)

        _normalise_rows(x_ref.shape[0], x_ref, norm_rows)

    first = 0
    for steps, use_scale, o_ref in zip(seg_steps, scaled, out_refs):
        @pl.when(jnp.logical_and(j >= first, j < first + steps))
        def _(use_scale=use_scale, o_ref=o_ref):
            acc = jnp.dot(h_ref[...], w_ref[...], preferred_element_type=F32)
            if use_scale:
                acc = acc * cs_ref[...]
            o_ref[...] = acc[:, :o_ref.shape[1]].astype(o_ref.dtype)

        first += steps


def _in_proj(x, g, w, col_scale, segments, name):
    t, d = x.shape
    tm = _pick(t, (1024, 512, 256, 128))
    tn = _pick(functools.reduce(math.gcd, [s[0] for s in segments]), (512, 256, 128))
    seg_steps = tuple(s[0] // tn for s in segments)
    assert sum(s[0] for s in segments) == w.shape[1]
    firsts = [sum(seg_steps[:k]) for k in range(len(segments))]
    out_shapes, out_specs = [], []
    for (n_w, n_out, dtype, _), first, steps in zip(segments, firsts, seg_steps):
        assert n_out == n_w or (steps == 1 and n_out <= tn)
        bn = tn if n_out == n_w else n_out
        out_shapes.append(jax.ShapeDtypeStruct((t, n_out), dtype))
        out_specs.append(pl.BlockSpec(
            (tm, bn), lambda i, j, first=first, steps=steps: (i, jnp.clip(j - first, 0, steps - 1))))
    return pl.pallas_call(
        functools.partial(_in_proj_kernel, seg_steps, tuple(s[3] for s in segments)),
        out_shape=out_shapes,
        grid=(t // tm, sum(seg_steps)),
        in_specs=[
            pl.BlockSpec((tm, d), lambda i, j: (i, 0)),
            pl.BlockSpec((1, d), lambda i, j: (0, 0)),
            pl.BlockSpec((d, tn), lambda i, j: (0, j)),
            pl.BlockSpec((1, tn), lambda i, j: (0, j)),
        ],
        out_specs=out_specs,
        scratch_shapes=[pltpu.VMEM((tm, d), BF16)],
        compiler_params=_params(("parallel", "arbitrary")),
        name=name,
    )(x, g, w, col_scale)


def _out_proj_kernel(na, ya_ref, yb_ref, w_ref, x_ref, g_ref, o_ref):
    k = pl.program_id(1)
    tm = x_ref.shape[0]

    @pl.when(k == 0)
    def _():
        o_ref[...] = jnp.dot(ya_ref[...], w_ref[...], preferred_element_type=F32)

    @pl.when(jnp.logical_and(k > 0, k < na))
    def _():
        o_ref[...] += jnp.dot(ya_ref[...], w_ref[...], preferred_element_type=F32)

    @pl.when(k >= na)
    def _():
        o_ref[...] += jnp.dot(yb_ref[...], w_ref[...], preferred_element_type=F32)

    @pl.when(k == pl.num_programs(1) - 1)
    def _():
        def finish_rows(rows, scale):
            o_ref[rows, :] = x_ref[rows, :] + o_ref[rows, :] * scale * g_ref[...]

        _normalise_rows(tm, o_ref, finish_rows)


def _out_proj(ya, yb, w, x, g):
    t, d = x.shape
    wa, wb = ya.shape[1], yb.shape[1]
    tm = _pick(t, (512, 256, 128))
    tk = _pick(math.gcd(wa, wb), (1024, 512, 256, 128))
    na, nb = wa // tk, wb // tk
    return pl.pallas_call(
        functools.partial(_out_proj_kernel, na),
        out_shape=jax.ShapeDtypeStruct((t, d), F32),
        grid=(t // tm, na + nb),
        in_specs=[
            pl.BlockSpec((tm, tk), lambda i, k: (i, jnp.minimum(k, na - 1))),
            pl.BlockSpec((tm, tk), lambda i, k: (i, jnp.maximum(k - na, 0))),
            pl.BlockSpec((tk, d), lambda i, k: (k, 0)),
            pl.BlockSpec((tm, d), lambda i, k: (i, 0)),
            pl.BlockSpec((1, d), lambda i, k: (0, 0)),
        ],
        out_specs=pl.BlockSpec((tm, d), lambda i, k: (i, 0)),
        compiler_params=_params(("parallel", "arbitrary")),
        name="out_proj",
    )(ya, yb, w, x, g)


def _conv_kernel(gb_ref, gc_ref, xt_ref, cw_ref, o_ref, tail_ref):
    ts = gb_ref.shape[0]
    width = cw_ref.shape[0]
    hist = tail_ref.shape[0]

    @pl.when(pl.program_id(1) == 0)
    def _():
        tail_ref[...] = jnp.zeros_like(tail_ref)

    u = gc_ref[...] * xt_ref[...]
    row = lax.broadcasted_iota(jnp.int32, u.shape, 0)
    acc = cw_ref[width - 1:width, :] * u
    for back in range(1, width):
        shifted = pltpu.roll(u, back, 0)
        for r in range(back):
            prev = tail_ref[hist - back + r:hist - back + r + 1, :]
            shifted = jnp.where(row == r, prev, shifted)
        acc = acc + cw_ref[width - 1 - back:width - back, :] * shifted
    o_ref[...] = (gb_ref[...] * acc).astype(o_ref.dtype)
    tail_ref[...] = u[ts - hist:, :]


def _short_conv(pa, conv_w, batch, seq):
    t = pa.shape[0]
    c = conv_w.shape[1]
    ts = _pick(seq, (256, 128))
    tc = _pick(c, (512, 256, 128))
    nc = c // tc
    ns = seq // ts
    return pl.pallas_call(
        _conv_kernel,
        out_shape=jax.ShapeDtypeStruct((t, c), BF16),
        grid=(batch * nc, ns),
        in_specs=[
            pl.BlockSpec((ts, tc), lambda bj, s: ((bj // nc) * ns + s, bj % nc)),
            pl.BlockSpec((ts, tc), lambda bj, s: ((bj // nc) * ns + s, nc + bj % nc)),
            pl.BlockSpec((ts, tc), lambda bj, s: ((bj // nc) * ns + s, 2 * nc + bj % nc)),
            pl.BlockSpec((conv_w.shape[0], tc), lambda bj, s: (0, bj % nc)),
        ],
        out_specs=pl.BlockSpec((ts, tc), lambda bj, s: ((bj // nc) * ns + s, bj % nc)),
        scratch_shapes=[pltpu.VMEM((8, tc), F32)],
        compiler_params=_params(("parallel", "arbitrary")),
        name="short_conv",
    )(pa, pa, pa, conv_w)


def _transposed_bf16(block):
    return block.astype(F32).T.astype(BF16)


def _init_softmax_state(m_ref, l_ref, acc_ref, slot):
    m_ref[slot] = jnp.full(m_ref.shape[1:], NEG_BIG, F32)
    l_ref[slot] = jnp.zeros(l_ref.shape[1:], F32)
    acc_ref[slot] = jnp.zeros(acc_ref.shape[1:], F32)


def _softmax_chunk(s, off, masked, v_t, m_ref, l_ref, acc_ref, slot):
    if masked:
        key = lax.broadcasted_iota(jnp.int32, s.shape, 0)
        qry = lax.broadcasted_iota(jnp.int32, s.shape, 1)
        s = jnp.where(qry >= key, s, NEG_BIG)
    m_prev = m_ref[slot]
    m_new = jnp.maximum(m_prev, jnp.max(s, axis=0, keepdims=True) + off)
    alpha = jnp.exp2(m_prev - m_new)
    p = jnp.exp2(s - (m_new - off))
    l_ref[slot] = alpha * l_ref[slot] + jnp.sum(p, axis=0, keepdims=True)
    acc_ref[slot] = alpha * acc_ref[slot] + jnp.dot(v_t, p.astype(BF16), preferred_element_type=F32)
    m_ref[slot] = m_new


def _causal_chunks(qi, tq, chunk):
    def body(j, carry):
        chunk(pl.multiple_of(j * (2 * tq), 2 * tq), 2 * tq, False)
        return carry

    lax.fori_loop(0, qi // 2, body, 0)

    @pl.when(qi % 2 == 1)
    def _():
        chunk(pl.multiple_of((qi - 1) * tq, 2 * tq), tq, False)

    chunk(pl.multiple_of(qi * tq, tq), tq, True)


def _diff_attn_kernel(lam_init, dh, slopes_ref, lam_ref, g_ref, q_ref, k_ref, v_ref, o_ref,
                      qt_ref, vt_ref, bias_ref, m_ref, l_ref, acc_ref):
    h = pl.program_id(1)
    qi = pl.program_id(2)
    tq = q_ref.shape[0]
    seq = k_ref.shape[0]

    slope2 = slopes_ref[h] * LOG2E

    @pl.when(qi == 0)
    def _():
        for c in range(seq // tq):
            rows = slice(c * tq, (c + 1) * tq)
            vt_ref[:, rows] = _transposed_bf16(v_ref[rows, :])
        bias_ref[...] = lax.broadcasted_iota(jnp.int32, bias_ref.shape, 0).astype(F32) * slope2

    for mp in range(2):
        qt_ref[mp] = _transposed_bf16(q_ref[:, mp * dh:(mp + 1) * dh])
        _init_softmax_state(m_ref, l_ref, acc_ref, mp)

    def chunk(start, n, masked):
        off = ((start - qi * tq) + jnp.zeros((1, tq), jnp.int32)).astype(F32) * slope2
        v_t = vt_ref[:, pl.ds(start, n)]
        for mp in range(2):
            k = k_ref[pl.ds(start, n), mp * dh:(mp + 1) * dh]
            s = jnp.dot(k, qt_ref[mp], preferred_element_type=F32) + bias_ref[0:n, :]
            _softmax_chunk(s, off, masked, v_t, m_ref, l_ref, acc_ref, mp)

    _causal_chunks(qi, tq, chunk)

    lp = lam_ref[...]
    lam = (jnp.exp(jnp.sum(lp[0:1] * lp[1:2], axis=-1, keepdims=True))
           - jnp.exp(jnp.sum(lp[2:3] * lp[3:4], axis=-1, keepdims=True)) + lam_init)
    o_t = acc_ref[0] * (1.0 / l_ref[0]) - lam * (acc_ref[1] * (1.0 / l_ref[1]))
    rms = lax.rsqrt(jnp.mean(o_t * o_t, axis=0, keepdims=True) + EPS)
    o_ref[...] = ((o_t * rms).T * g_ref[...] * (1.0 - lam_init)).astype(o_ref.dtype)


def _alibi_slopes(n):
    def pow2(m):
        start = 2.0 ** (-8.0 / m)
        return [start ** (i + 1) for i in range(m)]

    if math.log2(n).is_integer():
        s = pow2(n)
    else:
        c = 2 ** int(math.floor(math.log2(n)))
        s = pow2(c) + pow2(2 * c)[0::2][: n - c]
    return jnp.asarray(np.array(s, dtype=np.float32))


def _diff_attention(qkv, lam_p, subln_g, lam_init, batch, seq, heads, dh):
    t = qkv.shape[0]
    dv = 2 * dh
    tq = _pick(seq, (512, 256, 128))
    nq = seq // tq
    return pl.pallas_call(
        functools.partial(_diff_attn_kernel, lam_init, dh),
        out_shape=jax.ShapeDtypeStruct((t, heads * dv), BF16),
        grid=(batch, heads, nq),
        in_specs=[
            pl.BlockSpec(memory_space=pltpu.SMEM),
            pl.BlockSpec(lam_p.shape, lambda b, h, i: (0, 0)),
            pl.BlockSpec((1, dv), lambda b, h, i: (0, 0)),
            pl.BlockSpec((tq, dv), lambda b, h, i: (b * nq + i, h)),
            pl.BlockSpec((seq, dv), lambda b, h, i: (b, heads + h)),
            pl.BlockSpec((seq, dv), lambda b, h, i: (b, 2 * heads + h)),
        ],
        out_specs=pl.BlockSpec((tq, dv), lambda b, h, i: (b * nq + i, h)),
        scratch_shapes=[
            pltpu.VMEM((2, dh, tq), BF16),
            pltpu.VMEM((dv, seq), BF16),
            pltpu.VMEM((2 * tq, tq), F32),
            pltpu.VMEM((2, 1, tq), F32),
            pltpu.VMEM((2, 1, tq), F32),
            pltpu.VMEM((2, dv, tq), F32),
        ],
        compiler_params=_params(("parallel", "parallel", "arbitrary")),
        name="diff_attn",
    )(_alibi_slopes(heads), lam_p, subln_g, qkv, qkv, qkv)


def _fox_attn_kernel(dh, cum_ref, q_ref, k_ref, v_ref, o_ref,
                     qt_ref, vt_ref, rel_ref, first_ref, m_ref, l_ref, acc_ref):
    pair = pl.program_id(1)
    qi = pl.program_id(2)
    tq = q_ref.shape[0]
    seq = k_ref.shape[0]
    reps = tq // LANES

    span = 2 * tq

    @pl.when(qi == 0)
    def _():
        lane = lax.broadcasted_iota(jnp.int32, (span, LANES), 1)
        for c in range(seq // span):
            rows = slice(c * span, (c + 1) * span)
            cum = cum_ref[rows, :]
            for hh in range(2):
                vt_ref[hh, :, rows] = _transposed_bf16(v_ref[rows, hh * dh:(hh + 1) * dh])
                col = jnp.sum(jnp.where(lane == 2 * pair + hh, cum, 0.0), axis=1, keepdims=True)
                first = col[0:1, :]
                rel_ref[hh, rows, :] = jnp.broadcast_to((first - col) * LOG2E, (span, LANES))
                first_ref[hh, c:c + 1, :] = jnp.broadcast_to(first, (1, LANES))

    for hh in range(2):
        qt_ref[hh] = _transposed_bf16(q_ref[:, hh * dh:(hh + 1) * dh])
        _init_softmax_state(m_ref, l_ref, acc_ref, hh)

    def chunk(start, n, masked):
        for hh in range(2):
            k = k_ref[pl.ds(start, n), hh * dh:(hh + 1) * dh]
            rel = rel_ref[hh, pl.ds(start, n), :]
            s = jnp.dot(k, qt_ref[hh], preferred_element_type=F32) + jnp.concatenate([rel] * reps, axis=1)
            off = (first_ref[hh, pl.ds(qi // 2, 1), :] - first_ref[hh, pl.ds(start // span, 1), :]) * LOG2E
            off = jnp.concatenate([off] * reps, axis=1)
            _softmax_chunk(s, off, masked, vt_ref[hh, :, pl.ds(start, n)], m_ref, l_ref, acc_ref, hh)

    _causal_chunks(qi, tq, chunk)

    for hh in range(2):
        o_t = acc_ref[hh] * (1.0 / l_ref[hh])
        o_ref[:, hh * dh:(hh + 1) * dh] = o_t.T.astype(o_ref.dtype)


def _fox_attention(qkv, cum, batch, seq, heads, dh):
    t = qkv.shape[0]
    assert heads % 2 == 0
    pairs = heads // 2
    tq = _pick(seq, (512, 256, 128))
    nq = seq // tq
    assert nq % 2 == 0
    return pl.pallas_call(
        functools.partial(_fox_attn_kernel, dh),
        out_shape=jax.ShapeDtypeStruct((t, heads * dh), BF16),
        grid=(batch, pairs, nq),
        in_specs=[
            pl.BlockSpec((seq, LANES), lambda b, h, i: (b, 0)),
            pl.BlockSpec((tq, 2 * dh), lambda b, h, i: (b * nq + i, h)),
            pl.BlockSpec((seq, 2 * dh), lambda b, h, i: (b, pairs + h)),
            pl.BlockSpec((seq, 2 * dh), lambda b, h, i: (b, 2 * pairs + h)),
        ],
        out_specs=pl.BlockSpec((tq, 2 * dh), lambda b, h, i: (b * nq + i, h)),
        scratch_shapes=[
            pltpu.VMEM((2, dh, tq), BF16),
            pltpu.VMEM((2, dh, seq), BF16),
            pltpu.VMEM((2, seq, LANES), F32),
            pltpu.VMEM((2, nq, LANES), F32),
            pltpu.VMEM((2, 1, tq), F32),
            pltpu.VMEM((2, 1, tq), F32),
            pltpu.VMEM((2, dh, tq), F32),
        ],
        compiler_params=_params(("parallel", "parallel", "arbitrary")),
        name="fox_attn",
    )(cum, qkv, qkv, qkv)


def _forget_cumsum_kernel(f_ref, b_ref, o_ref):
    seq = f_ref.shape[0]
    ck = LANES
    r = lax.broadcasted_iota(jnp.int32, (ck, ck), 0)
    c = lax.broadcasted_iota(jnp.int32, (ck, ck), 1)
    tril = (r >= c).astype(F32)

    def body(i, carry):
        rows = pl.ds(pl.multiple_of(i * ck, ck), ck)
        z = f_ref[rows, :] + b_ref[...]
        log_f = jnp.minimum(z, 0.0) - jnp.log1p(jnp.exp(-jnp.abs(z)))
        cs = jnp.dot(tril, log_f, preferred_element_type=F32, precision=lax.Precision.HIGHEST) + carry
        o_ref[rows, :] = cs
        return cs[ck - 1:ck, :]

    lax.fori_loop(0, seq // ck, body, jnp.zeros((1, ck), F32))


def _forget_cumsum(f_proj, f_bias, batch, seq):
    return pl.pallas_call(
        _forget_cumsum_kernel,
        out_shape=jax.ShapeDtypeStruct(f_proj.shape, F32),
        grid=(batch,),
        in_specs=[
            pl.BlockSpec((seq, LANES), lambda b: (b, 0)),
            pl.BlockSpec((1, LANES), lambda b: (0, 0)),
        ],
        out_specs=pl.BlockSpec((seq, LANES), lambda b: (b, 0)),
        compiler_params=_params(("parallel",)),
        name="forget_cumsum",
    )(f_proj, f_bias)


def _sgu_kernel(u_ref, z_ref, w_ref, b_ref, lng_ref, lnb_ref, o_ref):
    rows, width = u_ref.shape
    groups, ck, _ = w_ref.shape
    gw = width // groups
    r = lax.broadcasted_iota(jnp.int32, (ck, ck), 0)
    c = lax.broadcasted_iota(jnp.int32, (ck, ck), 1)
    w_tri = [jnp.where(r >= c, w_ref[g], 0.0).astype(BF16) for g in range(groups)]
    for ci in range(rows // ck):
        rs = slice(ci * ck, (ci + 1) * ck)
        z = _gelu_tanh(z_ref[rs, :])
        mu = jnp.mean(z, axis=-1, keepdims=True)
        zc = z - mu
        var = jnp.mean(zc * zc, axis=-1, keepdims=True)
        zn = (zc * lax.rsqrt(var + EPS) * lng_ref[...] + lnb_ref[...]).astype(BF16)
        for g in range(groups):
            cs = slice(g * gw, (g + 1) * gw)
            mixed = jnp.dot(w_tri[g], zn[:, cs], preferred_element_type=F32) + b_ref[g]
            o_ref[rs, cs] = (_gelu_tanh(u_ref[rs, cs]) * mixed).astype(o_ref.dtype)


def _sgu(uz, sgu_w, sgu_b, ln_g, ln_b):
    t = uz.shape[0]
    groups, ck, _ = sgu_w.shape
    width = ln_g.shape[1]
    rows = _pick(t, (2 * ck, ck))
    return pl.pallas_call(
        _sgu_kernel,
        out_shape=jax.ShapeDtypeStruct((t, width), BF16),
        grid=(t // rows,),
        in_specs=[
            pl.BlockSpec((rows, width), lambda i: (i, 0)),
            pl.BlockSpec((rows, width), lambda i: (i, 1)),
            pl.BlockSpec(sgu_w.shape, lambda i: (0, 0, 0)),
            pl.BlockSpec(sgu_b.shape, lambda i: (0, 0, 0)),
            pl.BlockSpec((1, width), lambda i: (0, 0)),
            pl.BlockSpec((1, width), lambda i: (0, 0)),
        ],
        out_specs=pl.BlockSpec((rows, width), lambda i: (i, 0)),
        compiler_params=_params(("parallel",)),
        name="sgu",
    )(uz, uz, sgu_w, sgu_b, ln_g, ln_b)


def _query_col_scale(n, q_first, n_query, dh):
    col = jnp.arange(n)
    is_query = jnp.logical_and(col >= q_first, col < q_first + n_query)
    return jnp.where(is_query, dh ** -0.5 * LOG2E, 1.0).astype(F32)[None, :]


def _even_mixer(x, g_pre, g_post, w_in, w_out, conv_w, lam_p, subln_g, layer_idx, batch, seq):
    d = x.shape[1]
    conv_ch = conv_w.shape[1]
    dh = lam_p.shape[1]
    dv = subln_g.shape[0]
    heads = (d - conv_ch) // dv
    n_in = w_in.shape[1]
    n_attn = n_in - 3 * conv_ch
    pa, qkv = _in_proj(
        x, g_pre, w_in.astype(BF16), _query_col_scale(n_in, 3 * conv_ch, 2 * heads * dh, dh),
        ((3 * conv_ch, 3 * conv_ch, F32, False), (n_attn, n_attn, BF16, True)), "even_in")
    ya = _short_conv(pa, conv_w, batch, seq)
    lam_init = 0.8 - 0.6 * math.exp(-0.3 * layer_idx)
    yb = _diff_attention(qkv, lam_p, subln_g[None, :], lam_init, batch, seq, heads, dh)
    return _out_proj(ya, yb, w_out.astype(BF16), x, g_post)


def _odd_mixer(x, g_pre, g_post, w_in, w_out, sgu_w, sgu_b, ln_g, ln_b, f_bias, batch, seq):
    d = x.shape[1]
    width = ln_g.shape[0]
    heads = f_bias.shape[0]
    dh = (d - width) // heads
    n_attn = 3 * heads * dh
    assert heads <= LANES
    gate_cols = _pick(math.gcd(2 * width, n_attn), (512, 256, 128))
    n_in = 2 * width + n_attn + gate_cols
    w_bf = jnp.pad(w_in.astype(BF16), ((0, 0), (0, n_in - w_in.shape[1])))
    uz, qkv, f_proj = _in_proj(
        x, g_pre, w_bf, _query_col_scale(n_in, 2 * width, heads * dh, dh),
        ((2 * width, 2 * width, F32, False), (n_attn, n_attn, BF16, True), (gate_cols, LANES, F32, False)),
        "odd_in")
    bias =jnp.pad(f_bias, (0, LANES - heads))[None, :]
    cum = _forget_cumsum(f_proj, bias, batch, seq)
    yc = _sgu(uz, sgu_w, sgu_b[:, :, None], ln_g[None, :], ln_b[None, :])
    yd = _fox_attention(qkv, cum, batch, seq, heads, dh)
    return _out_proj(yc, yd, w_out.astype(BF16), x, g_post)


def kernel(x, norm_g, ffn_w_in, ffn_w_out, even_w_in, even_w_out, conv_w, diff_lambda, diff_subln_g,
           odd_w_in, odd_w_out, sgu_w, sgu_b, sgu_ln_g, sgu_ln_b, fox_f_bias):
    batch, seq, d = x.shape
    depth = norm_g.shape[0]
    xt = x.reshape(batch * seq, d)
    ffn_w_in = ffn_w_in.astype(BF16)
    ffn_w_out = ffn_w_out.astype(BF16)
    for l in range(depth):
        g = norm_g[l][:, None, :]
        xt = _ffn(xt, g[0], g[1], ffn_w_in, ffn_w_out, l, 0)
        i = l // 2
        if l % 2 == 0:
            xt = _even_mixer(xt, g[2], g[3], even_w_in[i], even_w_out[i], conv_w[i], diff_lambda[i],
                             diff_subln_g[i], l, batch, seq)
        else:
            xt = _odd_mixer(xt, g[2], g[3], odd_w_in[i], odd_w_out[i], sgu_w[i], sgu_b[i],
                            sgu_ln_g[i], sgu_ln_b[i], fox_f_bias[i], batch, seq)
        xt = _ffn(xt, g[4], g[5], ffn_w_in, ffn_w_out, l, 1)
    return xt.reshape(batch, seq, d)
```

```python
import functools
import math

import jax
import jax.numpy as jnp
import numpy as np
from jax import lax
from jax.experimental import pallas as pl
from jax.experimental.pallas import tpu as pltpu

EPS = 1e-6
F32 = jnp.float32
BF16 = jnp.bfloat16
NEG_BIG = -1e30
V7X_VMEM_LIMIT_BYTES = 60000 * 1024
LANES = 128
LOG2E = math.log2(math.e)
NORM_ROWS = 128
NORM_SUB_ROWS = 16


def _params(semantics):
    return pltpu.CompilerParams(dimension_semantics=semantics, vmem_limit_bytes=V7X_VMEM_LIMIT_BYTES)


def _pick(n, cands):
    for c in cands:
        if n % c == 0:
            return c
    raise ValueError(f"no tile in {cands} divides {n}")


def _rms_scale(v):
    return lax.rsqrt(jnp.mean(v * v, axis=-1, keepdims=True) + EPS)


def _normalise_rows(n_rows, src_ref, apply):
    step = min(NORM_ROWS, n_rows)
    sub = min(NORM_SUB_ROWS, step)
    assert n_rows % step == 0 and step % sub == 0

    def it(c, carry):
        first = pl.multiple_of(c * step, step)
        scale = _rms_scale(src_ref[pl.ds(first, step), :])
        for k in range(step // sub):
            apply(pl.ds(first + k * sub, sub), scale[k * sub:(k + 1) * sub])
        return carry

    lax.fori_loop(0, n_rows // step, it, 0)


def _gelu_tanh(v):
    return 0.5 * v * (1.0 + jnp.tanh(math.sqrt(2.0 / math.pi) * (v + 0.044715 * (v * v * v))))


def _ffn_kernel(cast_next, x_ref, gpre_ref, gpost_ref, wg_ref, wu_ref, wo_ref, *refs):
    if cast_next:
        nin_ref, nout_ref, o_ref, nin_bf_ref, nout_bf_ref, h_ref = refs
    else:
        o_ref, h_ref = refs
    f = pl.program_id(1)
    tm = x_ref.shape[0]

    @pl.when(f == 0)
    def _():
        def norm_rows(rows, scale):
            h_ref[rows, :] = (x_ref[rows, :] * scale * gpre_ref[...]).astype(BF16)
            o_ref[rows, :] = jnp.zeros((rows.size, x_ref.shape[1]), F32)

        _normalise_rows(tm, x_ref, norm_rows)

    h = h_ref[...]
    gate = jnp.dot(h, wg_ref[...], preferred_element_type=F32)
    up = jnp.dot(h, wu_ref[...], preferred_element_type=F32)
    act = (gate / (1.0 + jnp.exp(-gate)) * up).astype(BF16)
    o_ref[...] += jnp.dot(act, wo_ref[...], preferred_element_type=F32)
    if cast_next:
        nin_bf_ref[...] = nin_ref[...].astype(BF16)
        nout_bf_ref[...] = nout_ref[...].astype(BF16)

    @pl.when(f == pl.num_programs(1) - 1)
    def _():
        def finish_rows(rows, scale):
            o_ref[rows, :] = x_ref[rows, :] + o_ref[rows, :] * (0.5 * scale) * gpost_ref[...]

        _normalise_rows(tm, o_ref, finish_rows)


def _ffn(x, g_pre, g_post, w_in, w_out, next_weights=None):
    t, d = x.shape
    d_ff = w_out.shape[0]
    tm = _pick(t, (512, 256, 128))
    tf = _pick(d_ff, (256, 128))
    nt, nf = t // tm, d_ff // tf
    in_specs = [
        pl.BlockSpec((tm, d), lambda i, f: (i, 0)),
        pl.BlockSpec((1, d), lambda i, f: (0, 0)),
        pl.BlockSpec((1, d), lambda i, f: (0, 0)),
        pl.BlockSpec((d, tf), lambda i, f: (0, f)),
        pl.BlockSpec((d, tf), lambda i, f: (0, f + nf)),
        pl.BlockSpec((tf, d), lambda i, f: (f, 0)),
    ]
    out_shape = [jax.ShapeDtypeStruct((t, d), F32)]
    out_specs = [pl.BlockSpec((tm, d), lambda i, f: (i, 0))]
    operands = [x, g_pre, g_post, w_in, w_in, w_out]
    if next_weights is not None:
        nw_in, nw_out, layer, half = next_weights
        assert d % nt == 0 and (d // nt) % LANES == 0 and (2 * tf) % LANES == 0
        in_blk, out_blk = (d // nt, 2 * tf), (tf, d // nt)
        in_specs += [
            pl.BlockSpec((None, None) + in_blk, lambda i, f: (layer, half, i, f)),
            pl.BlockSpec((None, None) + out_blk, lambda i, f: (layer, half, f, i)),
        ]
        out_shape += [jax.ShapeDtypeStruct(nw_in.shape[2:], BF16), jax.ShapeDtypeStruct(nw_out.shape[2:], BF16)]
        out_specs += [pl.BlockSpec(in_blk, lambda i, f: (i, f)), pl.BlockSpec(out_blk, lambda i, f: (f, i))]
        operands += [nw_in, nw_out]
    return pl.pallas_call(
        functools.partial(_ffn_kernel, next_weights is not None),
        out_shape=out_shape,
        grid=(nt, nf),
        in_specs=in_specs,
        out_specs=out_specs,
        scratch_shapes=[pltpu.VMEM((tm, d), BF16)],
        compiler_params=_params(("parallel", "arbitrary")),
        name="ffn",
    )(*operands)


def _in_proj_kernel(seg_steps, scaled, x_ref, g_ref, w_ref, cs_ref, *refs):
    out_refs, h_ref = refs[:-1], refs[-1]
    j = pl.program_id(1)

    @pl.when(j == 0)
    def _():
        def norm_rows(rows, scale):
            h_ref[rows, :] = (x_ref[rows, :] * scale * g_ref[...]).astype(BF16)

        _normalise_rows(x_ref.shape[0], x_ref, norm_rows)

    first = 0
    for steps, use_scale, o_ref in zip(seg_steps, scaled, out_refs):
        @pl.when(jnp.logical_and(j >= first, j < first + steps))
        def _(use_scale=use_scale, o_ref=o_ref):
            acc = jnp.dot(h_ref[...], w_ref[...], preferred_element_type=F32)
            if use_scale:
                acc = acc * cs_ref[...]
            o_ref[...] = acc[:, :o_ref.shape[1]].astype(o_ref.dtype)

        first += steps


def _in_proj(x, g, w, col_scale, segments, name):
    t, d = x.shape
    tm = _pick(t, (1024, 512, 256, 128))
    tn = _pick(functools.reduce(math.gcd, [s[0] for s in segments]), (512, 256, 128))
    seg_steps = tuple(s[0] // tn for s in segments)
    assert sum(s[0] for s in segments) == w.shape[1]
    firsts = [sum(seg_steps[:k]) for k in range(len(segments))]
    out_shapes, out_specs = [], []
    for (n_w, n_out, dtype, _), first, steps in zip(segments, firsts, seg_steps):
        assert n_out == n_w or (steps == 1 and n_out <= tn)
        bn = tn if n_out == n_w else n_out
        out_shapes.append(jax.ShapeDtypeStruct((t, n_out), dtype))
        out_specs.append(pl.BlockSpec(
            (tm, bn), lambda i, j, first=first, steps=steps: (i, jnp.clip(j - first, 0, steps - 1))))
    return pl.pallas_call(
        functools.partial(_in_proj_kernel, seg_steps, tuple(s[3] for s in segments)),
        out_shape=out_shapes,
        grid=(t // tm, sum(seg_steps)),
        in_specs=[
            pl.BlockSpec((tm, d), lambda i, j: (i, 0)),
            pl.BlockSpec((1, d), lambda i, j: (0, 0)),
            pl.BlockSpec((d, tn), lambda i, j: (0, j)),
            pl.BlockSpec((1, tn), lambda i, j: (0, j)),
        ],
        out_specs=out_specs,
        scratch_shapes=[pltpu.VMEM((tm, d), BF16)],
        compiler_params=_params(("parallel", "arbitrary")),
        name=name,
    )(x, g, w, col_scale)


def _out_proj_kernel(na, ya_ref, yb_ref, w_ref, x_ref, g_ref, o_ref):
    k = pl.program_id(1)
    tm = x_ref.shape[0]

    @pl.when(k == 0)
    def _():
        o_ref[...] = jnp.dot(ya_ref[...], w_ref[...], preferred_element_type=F32)

    @pl.when(jnp.logical_and(k > 0, k < na))
    def _():
        o_ref[...] += jnp.dot(ya_ref[...], w_ref[...], preferred_element_type=F32)

    @pl.when(k >= na)
    def _():
        o_ref[...] += jnp.dot(yb_ref[...], w_ref[...], preferred_element_type=F32)

    @pl.when(k == pl.num_programs(1) - 1)
    def _():
        def finish_rows(rows, scale):
            o_ref[rows, :] = x_ref[rows, :] + o_ref[rows, :] * scale * g_ref[...]

        _normalise_rows(tm, o_ref, finish_rows)


def _out_proj(ya, yb, w, x, g):
    t, d = x.shape
    wa, wb = ya.shape[1], yb.shape[1]
    tm = _pick(t, (512, 256, 128))
    tk = _pick(math.gcd(wa, wb), (1024, 512, 256, 128))
    na, nb = wa // tk, wb // tk
    return pl.pallas_call(
        functools.partial(_out_proj_kernel, na),
        out_shape=jax.ShapeDtypeStruct((t, d), F32),
        grid=(t // tm, na + nb),
        in_specs=[
            pl.BlockSpec((tm, tk), lambda i, k: (i, jnp.minimum(k, na - 1))),
            pl.BlockSpec((tm, tk), lambda i, k: (i, jnp.maximum(k - na, 0))),
            pl.BlockSpec((tk, d), lambda i, k: (k, 0)),
            pl.BlockSpec((tm, d), lambda i, k: (i, 0)),
            pl.BlockSpec((1, d), lambda i, k: (0, 0)),
        ],
        out_specs=pl.BlockSpec((tm, d), lambda i, k: (i, 0)),
        compiler_params=_params(("parallel", "arbitrary")),
        name="out_proj",
    )(ya, yb, w, x, g)


def _conv_kernel(gb_ref, gc_ref, xt_ref, cw_ref, o_ref, tail_ref):
    ts = gb_ref.shape[0]
    width = cw_ref.shape[0]
    hist = tail_ref.shape[0]

    @pl.when(pl.program_id(1) == 0)
    def _():
        tail_ref[...] = jnp.zeros_like(tail_ref)

    u = gc_ref[...] * xt_ref[...]
    row = lax.broadcasted_iota(jnp.int32, u.shape, 0)
    acc = cw_ref[width - 1:width, :] * u
    for back in range(1, width):
        shifted = pltpu.roll(u, back, 0)
        for r in range(back):
            prev = tail_ref[hist - back + r:hist - back + r + 1, :]
            shifted = jnp.where(row == r, prev, shifted)
        acc = acc + cw_ref[width - 1 - back:width - back, :] * shifted
    o_ref[...] = (gb_ref[...] * acc).astype(o_ref.dtype)
    tail_ref[...] = u[ts - hist:, :]


def _short_conv(pa, conv_w, batch, seq):
    t = pa.shape[0]
    c = conv_w.shape[1]
    ts = _pick(seq, (256, 128))
    tc = _pick(c, (512, 256, 128))
    nc = c // tc
    ns = seq // ts
    return pl.pallas_call(
        _conv_kernel,
        out_shape=jax.ShapeDtypeStruct((t, c), BF16),
        grid=(batch * nc, ns),
        in_specs=[
            pl.BlockSpec((ts, tc), lambda bj, s: ((bj // nc) * ns + s, bj % nc)),
            pl.BlockSpec((ts, tc), lambda bj, s: ((bj // nc) * ns + s, nc + bj % nc)),
            pl.BlockSpec((ts, tc), lambda bj, s: ((bj // nc) * ns + s, 2 * nc + bj % nc)),
            pl.BlockSpec((conv_w.shape[0], tc), lambda bj, s: (0, bj % nc)),
        ],
        out_specs=pl.BlockSpec((ts, tc), lambda bj, s: ((bj // nc) * ns + s, bj % nc)),
        scratch_shapes=[pltpu.VMEM((8, tc), F32)],
        compiler_params=_params(("parallel", "arbitrary")),
        name="short_conv",
    )(pa, pa, pa, conv_w)


def _transposed_bf16(block):
    return block.astype(F32).T.astype(BF16)


def _init_softmax_state(m_ref, l_ref, acc_ref, slot):
    m_ref[slot] = jnp.full(m_ref.shape[1:], NEG_BIG, F32)
    l_ref[slot] = jnp.zeros(l_ref.shape[1:], F32)
    acc_ref[slot] = jnp.zeros(acc_ref.shape[1:], F32)


def _softmax_chunk(s, off, masked, v_t, m_ref, l_ref, acc_ref, slot):
    if masked:
        key = lax.broadcasted_iota(jnp.int32, s.shape, 0)
        qry = lax.broadcasted_iota(jnp.int32, s.shape, 1)
        s = jnp.where(qry >= key, s, NEG_BIG)
    m_prev = m_ref[slot]
    m_new = jnp.maximum(m_prev, jnp.max(s, axis=0, keepdims=True) + off)
    alpha = jnp.exp2(m_prev - m_new)
    p = jnp.exp2(s - (m_new - off))
    l_ref[slot] = alpha * l_ref[slot] + jnp.sum(p, axis=0, keepdims=True)
    acc_ref[slot] = alpha * acc_ref[slot] + jnp.dot(v_t, p.astype(BF16), preferred_element_type=F32)
    m_ref[slot] = m_new


def _causal_chunks(qi, tq, chunk):
    def body(j, carry):
        chunk(pl.multiple_of(j * (2 * tq), 2 * tq), 2 * tq, False)
        return carry

    lax.fori_loop(0, qi // 2, body, 0)

    @pl.when(qi % 2 == 1)
    def _():
        chunk(pl.multiple_of((qi - 1) * tq, 2 * tq), tq, False)

    chunk(pl.multiple_of(qi * tq, tq), tq, True)


def _diff_attn_kernel(lam_init, dh, slopes_ref, lam_ref, g_ref, q_ref, k_ref, v_ref, o_ref,
                      qt_ref, vt_ref, bias_ref, m_ref, l_ref, acc_ref):
    h = pl.program_id(1)
    qi = pl.program_id(2)
    tq = q_ref.shape[0]
    seq = k_ref.shape[0]

    slope2 = slopes_ref[h] * LOG2E

    @pl.when(qi == 0)
    def _():
        for c in range(seq // tq):
            rows = slice(c * tq, (c + 1) * tq)
            vt_ref[:, rows] = _transposed_bf16(v_ref[rows, :])
        bias_ref[...] = lax.broadcasted_iota(jnp.int32, bias_ref.shape, 0).astype(F32) * slope2

    for mp in range(2):
        qt_ref[mp] = _transposed_bf16(q_ref[:, mp * dh:(mp + 1) * dh])
        _init_softmax_state(m_ref, l_ref, acc_ref, mp)

    def chunk(start, n, masked):
        off = ((start - qi * tq) + jnp.zeros((1, tq), jnp.int32)).astype(F32) * slope2
        v_t = vt_ref[:, pl.ds(start, n)]
        for mp in range(2):
            k = k_ref[pl.ds(start, n), mp * dh:(mp + 1) * dh]
            s = jnp.dot(k, qt_ref[mp], preferred_element_type=F32) + bias_ref[0:n, :]
            _softmax_chunk(s, off, masked, v_t, m_ref, l_ref, acc_ref, mp)

    _causal_chunks(qi, tq, chunk)

    lp = lam_ref[...]
    lam = (jnp.exp(jnp.sum(lp[0:1] * lp[1:2], axis=-1, keepdims=True))
           - jnp.exp(jnp.sum(lp[2:3] * lp[3:4], axis=-1, keepdims=True)) + lam_init)
    o_t = acc_ref[0] * (1.0 / l_ref[0]) - lam * (acc_ref[1] * (1.0 / l_ref[1]))
    rms = lax.rsqrt(jnp.mean(o_t * o_t, axis=0, keepdims=True) + EPS)
    o_ref[...] = ((o_t * rms).T * g_ref[...] * (1.0 - lam_init)).astype(o_ref.dtype)


def _alibi_slopes(n):
    def pow2(m):
        start = 2.0 ** (-8.0 / m)
        return [start ** (i + 1) for i in range(m)]

    if math.log2(n).is_integer():
        s = pow2(n)
    else:
        c = 2 ** int(math.floor(math.log2(n)))
        s = pow2(c) + pow2(2 * c)[0::2][: n - c]
    return jnp.asarray(np.array(s, dtype=np.float32))


def _diff_attention(qkv, lam_p, subln_g, lam_init, batch, seq, heads, dh):
    t = qkv.shape[0]
    dv = 2 * dh
    tq = _pick(seq, (512, 256, 128))
    nq = seq // tq
    return pl.pallas_call(
        functools.partial(_diff_attn_kernel, lam_init, dh),
        out_shape=jax.ShapeDtypeStruct((t, heads * dv), BF16),
        grid=(batch, heads, nq),
        in_specs=[
            pl.BlockSpec(memory_space=pltpu.SMEM),
            pl.BlockSpec(lam_p.shape, lambda b, h, i: (0, 0)),
            pl.BlockSpec((1, dv), lambda b, h, i: (0, 0)),
            pl.BlockSpec((tq, dv), lambda b, h, i: (b * nq + i, h)),
            pl.BlockSpec((seq, dv), lambda b, h, i: (b, heads + h)),
            pl.BlockSpec((seq, dv), lambda b, h, i: (b, 2 * heads + h)),
        ],
        out_specs=pl.BlockSpec((tq, dv), lambda b, h, i: (b * nq + i, h)),
        scratch_shapes=[
            pltpu.VMEM((2, dh, tq), BF16),
            pltpu.VMEM((dv, seq), BF16),
            pltpu.VMEM((2 * tq, tq), F32),
            pltpu.VMEM((2, 1, tq), F32),
            pltpu.VMEM((2, 1, tq), F32),
            pltpu.VMEM((2, dv, tq), F32),
        ],
        compiler_params=_params(("parallel", "parallel", "arbitrary")),
        name="diff_attn",
    )(_alibi_slopes(heads), lam_p, subln_g, qkv, qkv, qkv)


def _fox_attn_kernel(dh, cum_ref, q_ref, k_ref, v_ref, o_ref,
                     qt_ref, vt_ref, rel_ref, first_ref, m_ref, l_ref, acc_ref):
    pair = pl.program_id(1)
    qi = pl.program_id(2)
    tq = q_ref.shape[0]
    seq = k_ref.shape[0]
    reps = tq // LANES

    span = 2 * tq

    @pl.when(qi == 0)
    def _():
        lane = lax.broadcasted_iota(jnp.int32, (span, LANES), 1)
        for c in range(seq // span):
            rows = slice(c * span, (c + 1) * span)
            cum = cum_ref[rows, :]
            for hh in range(2):
                vt_ref[hh, :, rows] = _transposed_bf16(v_ref[rows, hh * dh:(hh + 1) * dh])
                col = jnp.sum(jnp.where(lane == 2 * pair + hh, cum, 0.0), axis=1, keepdims=True)
                first = col[0:1, :]
                rel_ref[hh, rows, :] = jnp.broadcast_to((first - col) * LOG2E, (span, LANES))
                first_ref[hh, c:c + 1, :] = jnp.broadcast_to(first, (1, LANES))

    for hh in range(2):
        qt_ref[hh] = _transposed_bf16(q_ref[:, hh * dh:(hh + 1) * dh])
        _init_softmax_state(m_ref, l_ref, acc_ref, hh)

    def chunk(start, n, masked):
        for hh in range(2):
            k = k_ref[pl.ds(start, n), hh * dh:(hh + 1) * dh]
            rel = rel_ref[hh, pl.ds(start, n), :]
            s = jnp.dot(k, qt_ref[hh], preferred_element_type=F32) + jnp.concatenate([rel] * reps, axis=1)
            off = (first_ref[hh, pl.ds(qi // 2, 1), :] - first_ref[hh, pl.ds(start // span, 1), :]) * LOG2E
            off = jnp.concatenate([off] * reps, axis=1)
            _softmax_chunk(s, off, masked, vt_ref[hh, :, pl.ds(start, n)], m_ref, l_ref, acc_ref, hh)

    _causal_chunks(qi, tq, chunk)

    for hh in range(2):
        o_t = acc_ref[hh] * (1.0 / l_ref[hh])
        o_ref[:, hh * dh:(hh + 1) * dh] = o_t.T.astype(o_ref.dtype)


def _fox_attention(qkv, cum, batch, seq, heads, dh):
    t = qkv.shape[0]
    assert heads % 2 == 0
    pairs = heads // 2
    tq = _pick(seq, (512, 256, 128))
    nq = seq // tq
    assert nq % 2 == 0
    return pl.pallas_call(
        functools.partial(_fox_attn_kernel, dh),
        out_shape=jax.ShapeDtypeStruct((t, heads * dh), BF16),
        grid=(batch, pairs, nq),
        in_specs=[
            pl.BlockSpec((seq, LANES), lambda b, h, i: (b, 0)),
            pl.BlockSpec((tq, 2 * dh), lambda b, h, i: (b * nq + i, h)),
            pl.BlockSpec((seq, 2 * dh), lambda b, h, i: (b, pairs + h)),
            pl.BlockSpec((seq, 2 * dh), lambda b, h, i: (b, 2 * pairs + h)),
        ],
        out_specs=pl.BlockSpec((tq, 2 * dh), lambda b, h, i: (b * nq + i, h)),
        scratch_shapes=[
            pltpu.VMEM((2, dh, tq), BF16),
            pltpu.VMEM((2, dh, seq), BF16),
            pltpu.VMEM((2, seq, LANES), F32),
            pltpu.VMEM((2, nq, LANES), F32),
            pltpu.VMEM((2, 1, tq), F32),
            pltpu.VMEM((2, 1, tq), F32),
            pltpu.VMEM((2, dh, tq), F32),
        ],
        compiler_params=_params(("parallel", "parallel", "arbitrary")),
        name="fox_attn",
    )(cum, qkv, qkv, qkv)


def _forget_cumsum_kernel(f_ref, b_ref, o_ref):
    seq = f_ref.shape[0]
    ck = LANES
    r = lax.broadcasted_iota(jnp.int32, (ck, ck), 0)
    c = lax.broadcasted_iota(jnp.int32, (ck, ck), 1)
    tril = (r >= c).astype(F32)

    def body(i, carry):
        rows = pl.ds(pl.multiple_of(i * ck, ck), ck)
        z = f_ref[rows, :] + b_ref[...]
        log_f = jnp.minimum(z, 0.0) - jnp.log1p(jnp.exp(-jnp.abs(z)))
        cs = jnp.dot(tril, log_f, preferred_element_type=F32, precision=lax.Precision.HIGHEST) + carry
        o_ref[rows, :] = cs
        return cs[ck - 1:ck, :]

    lax.fori_loop(0, seq // ck, body, jnp.zeros((1, ck), F32))


def _forget_cumsum(f_proj, f_bias, batch, seq):
    return pl.pallas_call(
        _forget_cumsum_kernel,
        out_shape=jax.ShapeDtypeStruct(f_proj.shape, F32),
        grid=(batch,),
        in_specs=[
            pl.BlockSpec((seq, LANES), lambda b: (b, 0)),
            pl.BlockSpec((1, LANES), lambda b: (0, 0)),
        ],
        out_specs=pl.BlockSpec((seq, LANES), lambda b: (b, 0)),
        compiler_params=_params(("parallel",)),
        name="forget_cumsum",
    )(f_proj, f_bias)


def _sgu_kernel(u_ref, z_ref, w_ref, b_ref, lng_ref, lnb_ref, o_ref):
    rows, width = u_ref.shape
    groups, ck, _ = w_ref.shape
    gw = width // groups
    r = lax.broadcasted_iota(jnp.int32, (ck, ck), 0)
    c = lax.broadcasted_iota(jnp.int32, (ck, ck), 1)
    w_tri = [jnp.where(r >= c, w_ref[g], 0.0).astype(BF16) for g in range(groups)]
    for ci in range(rows // ck):
        rs = slice(ci * ck, (ci + 1) * ck)
        z = _gelu_tanh(z_ref[rs, :])
        mu = jnp.mean(z, axis=-1, keepdims=True)
        zc = z - mu
        var = jnp.mean(zc * zc, axis=-1, keepdims=True)
        zn = (zc * lax.rsqrt(var + EPS) * lng_ref[...] + lnb_ref[...]).astype(BF16)
        for g in range(groups):
            cs = slice(g * gw, (g + 1) * gw)
            mixed = jnp.dot(w_tri[g], zn[:, cs], preferred_element_type=F32) + b_ref[g]
            o_ref[rs, cs] = (_gelu_tanh(u_ref[rs, cs]) * mixed).astype(o_ref.dtype)


def _sgu(uz, sgu_w, sgu_b, ln_g, ln_b):
    t = uz.shape[0]
    groups, ck, _ = sgu_w.shape
    width = ln_g.shape[1]
    rows = _pick(t, (2 * ck, ck))
    return pl.pallas_call(
        _sgu_kernel,
        out_shape=jax.ShapeDtypeStruct((t, width), BF16),
        grid=(t // rows,),
        in_specs=[
            pl.BlockSpec((rows, width), lambda i: (i, 0)),
            pl.BlockSpec((rows, width), lambda i: (i, 1)),
            pl.BlockSpec(sgu_w.shape, lambda i: (0, 0, 0)),
            pl.BlockSpec(sgu_b.shape, lambda i: (0, 0, 0)),
            pl.BlockSpec((1, width), lambda i: (0, 0)),
            pl.BlockSpec((1, width), lambda i: (0, 0)),
        ],
        out_specs=pl.BlockSpec((rows, width), lambda i: (i, 0)),
        compiler_params=_params(("parallel",)),
        name="sgu",
    )(uz, uz, sgu_w, sgu_b, ln_g, ln_b)


def _query_col_scale(n, q_first, n_query, dh):
    col = jnp.arange(n)
    is_query = jnp.logical_and(col >= q_first, col < q_first + n_query)
    return jnp.where(is_query, dh ** -0.5 * LOG2E, 1.0).astype(F32)[None, :]


def _even_mixer(x, g_pre, g_post, w_in, w_out, conv_w, lam_p, subln_g, layer_idx, batch, seq):
    d = x.shape[1]
    conv_ch = conv_w.shape[1]
    dh = lam_p.shape[1]
    dv = subln_g.shape[0]
    heads = (d - conv_ch) // dv
    n_in = w_in.shape[1]
    n_attn = n_in - 3 * conv_ch
    pa, qkv = _in_proj(
        x, g_pre, w_in.astype(BF16), _query_col_scale(n_in, 3 * conv_ch, 2 * heads * dh, dh),
        ((3 * conv_ch, 3 * conv_ch, F32, False), (n_attn, n_attn, BF16, True)), "even_in")
    ya = _short_conv(pa, conv_w, batch, seq)
    lam_init = 0.8 - 0.6 * math.exp(-0.3 * layer_idx)
    yb = _diff_attention(qkv, lam_p, subln_g[None, :], lam_init, batch, seq, heads, dh)
    return _out_proj(ya, yb, w_out.astype(BF16), x, g_post)


def _odd_mixer(x, g_pre, g_post, w_in, w_out, sgu_w, sgu_b, ln_g, ln_b, f_bias, batch, seq):
    d = x.shape[1]
    width = ln_g.shape[0]
    heads = f_bias.shape[0]
    dh = (d - width) // heads
    n_attn = 3 * heads * dh
    assert heads <= LANES
    gate_cols = _pick(math.gcd(2 * width, n_attn), (512, 256, 128))
    n_in = 2 * width + n_attn + gate_cols
    w_bf = jnp.pad(w_in.astype(BF16), ((0, 0), (0, n_in - w_in.shape[1])))
    uz, qkv, f_proj = _in_proj(
        x, g_pre, w_bf, _query_col_scale(n_in, 2 * width, heads * dh, dh),
        ((2 * width, 2 * width, F32, False), (n_attn, n_attn, BF16, True), (gate_cols, LANES, F32, False)),
        "odd_in")
    bias =jnp.pad(f_bias, (0, LANES - heads))[None, :]
    cum = _forget_cumsum(f_proj, bias, batch, seq)
    yc = _sgu(uz, sgu_w, sgu_b[:, :, None], ln_g[None, :], ln_b[None, :])
    yd = _fox_attention(qkv, cum, batch, seq, heads, dh)
    return _out_proj(yc, yd, w_out.astype(BF16), x, g_post)


def kernel(x, norm_g, ffn_w_in, ffn_w_out, even_w_in, even_w_out, conv_w, diff_lambda, diff_subln_g,
           odd_w_in, odd_w_out, sgu_w, sgu_b, sgu_ln_g, sgu_ln_b, fox_f_bias):
    batch, seq, d = x.shape
    depth = norm_g.shape[0]
    xt = x.reshape(batch * seq, d)
    w_in, w_out = ffn_w_in[0, 0].astype(BF16), ffn_w_out[0, 0].astype(BF16)
    for l in range(depth):
        g = norm_g[l][:, None, :]
        xt, w_in, w_out = _ffn(xt, g[0], g[1], w_in, w_out, (ffn_w_in, ffn_w_out, l, 1))
        i = l // 2
        if l % 2 == 0:
            xt = _even_mixer(xt, g[2], g[3], even_w_in[i], even_w_out[i], conv_w[i], diff_lambda[i],
                             diff_subln_g[i], l, batch, seq)
        else:
            xt = _odd_mixer(xt, g[2], g[3], odd_w_in[i], odd_w_out[i], sgu_w[i], sgu_b[i],
                            sgu_ln_g[i], sgu_ln_b[i], fox_f_bias[i], batch, seq)
        if l + 1 < depth:
            xt, w_in, w_out = _ffn(xt, g[4], g[5], w_in, w_out, (ffn_w_in, ffn_w_out, l + 1, 0))
        else:
            (xt,) = _ffn(xt, g[4], g[5], w_in, w_out)
    return xt.reshape(batch, seq, d)
```

```python
import functools
import math

import jax
import jax.numpy as jnp
import numpy as np
from jax import lax
from jax.experimental import pallas as pl
from jax.experimental.pallas import tpu as pltpu

EPS = 1e-6
F32 = jnp.float32
BF16 = jnp.bfloat16
NEG_BIG = -1e30
V7X_VMEM_LIMIT_BYTES = 60000 * 1024
LANES = 128
LOG2E = math.log2(math.e)
NORM_ROWS = 128
NORM_SUB_ROWS = 16


def _params(semantics):
    return pltpu.CompilerParams(dimension_semantics=semantics, vmem_limit_bytes=V7X_VMEM_LIMIT_BYTES)


def _pick(n, cands):
    for c in cands:
        if n % c == 0:
            return c
    raise ValueError(f"no tile in {cands} divides {n}")


def _rms_scale(v):
    return lax.rsqrt(jnp.mean(v * v, axis=-1, keepdims=True) + EPS)


def _normalise_rows(n_rows, src_ref, apply):
    step = min(NORM_ROWS, n_rows)
    sub = min(NORM_SUB_ROWS, step)
    assert n_rows % step == 0 and step % sub == 0

    def it(c, carry):
        first = pl.multiple_of(c * step, step)
        scale = _rms_scale(src_ref[pl.ds(first, step), :])
        for k in range(step // sub):
            apply(pl.ds(first + k * sub, sub), scale[k * sub:(k + 1) * sub])
        return carry

    lax.fori_loop(0, n_rows // step, it, 0)


def _gelu_tanh(v):
    return 0.5 * v * (1.0 + jnp.tanh(math.sqrt(2.0 / math.pi) * (v + 0.044715 * (v * v * v))))


def _ffn_kernel(cast_next, x_ref, gpre_ref, gpost_ref, wg_ref, wu_ref, wo_ref, *refs):
    if cast_next:
        nin_ref, nout_ref, o_ref, nin_bf_ref, nout_bf_ref, h_ref, act_ref = refs
    else:
        o_ref, h_ref, act_ref = refs
    f = pl.program_id(1)
    nf = pl.num_programs(1) - 1
    tm = x_ref.shape[0]

    def hidden_tile():
        h = h_ref[...]
        gate = jnp.dot(h, wg_ref[...], preferred_element_type=F32)
        up = jnp.dot(h, wu_ref[...], preferred_element_type=F32)
        return (gate / (1.0 + jnp.exp(-gate)) * up).astype(BF16)

    def cast_next_tile():
        if cast_next:
            nin_bf_ref[...] = nin_ref[...].astype(BF16)
            nout_bf_ref[...] = nout_ref[...].astype(BF16)

    @pl.when(f == 0)
    def _():
        def norm_rows(rows, scale):
            h_ref[rows, :] = (x_ref[rows, :] * scale * gpre_ref[...]).astype(BF16)
            o_ref[rows, :] = jnp.zeros((rows.size, x_ref.shape[1]), F32)

        _normalise_rows(tm, x_ref, norm_rows)
        act_ref[...] = hidden_tile()
        cast_next_tile()

    @pl.when(jnp.logical_and(f > 0, f < nf))
    def _():
        prev = act_ref[...]
        o_ref[...] += jnp.dot(prev, wo_ref[...], preferred_element_type=F32)
        act_ref[...] = hidden_tile()
        cast_next_tile()

    @pl.when(f == nf)
    def _():
        o_ref[...] += jnp.dot(act_ref[...], wo_ref[...], preferred_element_type=F32)

        def finish_rows(rows, scale):
            o_ref[rows, :] = x_ref[rows, :] + o_ref[rows, :] * (0.5 * scale) * gpost_ref[...]

        _normalise_rows(tm, o_ref, finish_rows)


def _ffn(x, g_pre, g_post, w_in, w_out, next_weights=None):
    t, d = x.shape
    d_ff = w_out.shape[0]
    tm = _pick(t, (512, 256, 128))
    tf = _pick(d_ff, (256, 128))
    nt, nf = t // tm, d_ff // tf
    in_specs = [
        pl.BlockSpec((tm, d), lambda i, f: (i, 0)),
        pl.BlockSpec((1, d), lambda i, f: (0, 0)),
        pl.BlockSpec((1, d), lambda i, f: (0, 0)),
        pl.BlockSpec((d, tf), lambda i, f: (0, jnp.minimum(f, nf - 1))),
        pl.BlockSpec((d, tf), lambda i, f: (0, jnp.minimum(f, nf - 1) + nf)),
        pl.BlockSpec((tf, d), lambda i, f: (jnp.maximum(f - 1, 0), 0)),
    ]
    out_shape = [jax.ShapeDtypeStruct((t, d), F32)]
    out_specs = [pl.BlockSpec((tm, d), lambda i, f: (i, 0))]
    operands = [x, g_pre, g_post, w_in, w_in, w_out]
    if next_weights is not None:
        nw_in, nw_out, layer, half = next_weights
        assert d % nt == 0 and (d // nt) % LANES == 0 and (2 * tf) % LANES == 0
        in_blk, out_blk = (d // nt, 2 * tf), (tf, d // nt)
        in_specs += [
            pl.BlockSpec((None, None) + in_blk, lambda i, f: (layer, half, i, jnp.minimum(f, nf - 1))),
            pl.BlockSpec((None, None) + out_blk, lambda i, f: (layer, half, jnp.minimum(f, nf - 1), i)),
        ]
        out_shape += [jax.ShapeDtypeStruct(nw_in.shape[2:], BF16), jax.ShapeDtypeStruct(nw_out.shape[2:], BF16)]
        out_specs += [pl.BlockSpec(in_blk, lambda i, f: (i, jnp.minimum(f, nf - 1))),
                      pl.BlockSpec(out_blk, lambda i, f: (jnp.minimum(f, nf - 1), i))]
        operands += [nw_in, nw_out]
    return pl.pallas_call(
        functools.partial(_ffn_kernel, next_weights is not None),
        out_shape=out_shape,
        grid=(nt, nf + 1),
        in_specs=in_specs,
        out_specs=out_specs,
        scratch_shapes=[pltpu.VMEM((tm, d), BF16), pltpu.VMEM((tm, tf), BF16)],
        compiler_params=_params(("parallel", "arbitrary")),
        name="ffn",
    )(*operands)


def _in_proj_kernel(seg_steps, scaled, x_ref, g_ref, w_ref, cs_ref, *refs):
    out_refs, h_ref = refs[:-1], refs[-1]
    j = pl.program_id(1)

    @pl.when(j == 0)
    def _():
        def norm_rows(rows, scale):
            h_ref[rows, :] = (x_ref[rows, :] * scale * g_ref[...]).astype(BF16)

        _normalise_rows(x_ref.shape[0], x_ref, norm_rows)

    first = 0
    for steps, use_scale, o_ref in zip(seg_steps, scaled, out_refs):
        @pl.when(jnp.logical_and(j >= first, j < first + steps))
        def _(use_scale=use_scale, o_ref=o_ref):
            acc = jnp.dot(h_ref[...], w_ref[...], preferred_element_type=F32)
            if use_scale:
                acc = acc * cs_ref[...]
            o_ref[...] = acc[:, :o_ref.shape[1]].astype(o_ref.dtype)

        first += steps


def _in_proj(x, g, w, col_scale, segments, name):
    t, d = x.shape
    tm = _pick(t, (1024, 512, 256, 128))
    tn = _pick(functools.reduce(math.gcd, [s[0] for s in segments]), (512, 256, 128))
    seg_steps = tuple(s[0] // tn for s in segments)
    assert sum(s[0] for s in segments) == w.shape[1]
    firsts = [sum(seg_steps[:k]) for k in range(len(segments))]
    out_shapes, out_specs = [], []
    for (n_w, n_out, dtype, _), first, steps in zip(segments, firsts, seg_steps):
        assert n_out == n_w or (steps == 1 and n_out <= tn)
        bn = tn if n_out == n_w else n_out
        out_shapes.append(jax.ShapeDtypeStruct((t, n_out), dtype))
        out_specs.append(pl.BlockSpec(
            (tm, bn), lambda i, j, first=first, steps=steps: (i, jnp.clip(j - first, 0, steps - 1))))
    return pl.pallas_call(
        functools.partial(_in_proj_kernel, seg_steps, tuple(s[3] for s in segments)),
        out_shape=out_shapes,
        grid=(t // tm, sum(seg_steps)),
        in_specs=[
            pl.BlockSpec((tm, d), lambda i, j: (i, 0)),
            pl.BlockSpec((1, d), lambda i, j: (0, 0)),
            pl.BlockSpec((d, tn), lambda i, j: (0, j)),
            pl.BlockSpec((1, tn), lambda i, j: (0, j)),
        ],
        out_specs=out_specs,
        scratch_shapes=[pltpu.VMEM((tm, d), BF16)],
        compiler_params=_params(("parallel", "arbitrary")),
        name=name,
    )(x, g, w, col_scale)


def _out_proj_kernel(na, ya_ref, yb_ref, w_ref, x_ref, g_ref, o_ref):
    k = pl.program_id(1)
    tm = x_ref.shape[0]

    @pl.when(k == 0)
    def _():
        o_ref[...] = jnp.dot(ya_ref[...], w_ref[...], preferred_element_type=F32)

    @pl.when(jnp.logical_and(k > 0, k < na))
    def _():
        o_ref[...] += jnp.dot(ya_ref[...], w_ref[...], preferred_element_type=F32)

    @pl.when(k >= na)
    def _():
        o_ref[...] += jnp.dot(yb_ref[...], w_ref[...], preferred_element_type=F32)

    @pl.when(k == pl.num_programs(1) - 1)
    def _():
        def finish_rows(rows, scale):
            o_ref[rows, :] = x_ref[rows, :] + o_ref[rows, :] * scale * g_ref[...]

        _normalise_rows(tm, o_ref, finish_rows)


def _out_proj(ya, yb, w, x, g):
    t, d = x.shape
    wa, wb = ya.shape[1], yb.shape[1]
    tm = _pick(t, (512, 256, 128))
    tk = _pick(math.gcd(wa, wb), (1024, 512, 256, 128))
    na, nb = wa // tk, wb // tk
    return pl.pallas_call(
        functools.partial(_out_proj_kernel, na),
        out_shape=jax.ShapeDtypeStruct((t, d), F32),
        grid=(t // tm, na + nb),
        in_specs=[
            pl.BlockSpec((tm, tk), lambda i, k: (i, jnp.minimum(k, na - 1))),
            pl.BlockSpec((tm, tk), lambda i, k: (i, jnp.maximum(k - na, 0))),
            pl.BlockSpec((tk, d), lambda i, k: (k, 0)),
            pl.BlockSpec((tm, d), lambda i, k: (i, 0)),
            pl.BlockSpec((1, d), lambda i, k: (0, 0)),
        ],
        out_specs=pl.BlockSpec((tm, d), lambda i, k: (i, 0)),
        compiler_params=_params(("parallel", "arbitrary")),
        name="out_proj",
    )(ya, yb, w, x, g)


def _conv_kernel(gb_ref, gc_ref, xt_ref, cw_ref, o_ref, tail_ref):
    ts = gb_ref.shape[0]
    width = cw_ref.shape[0]
    hist = tail_ref.shape[0]

    @pl.when(pl.program_id(1) == 0)
    def _():
        tail_ref[...] = jnp.zeros_like(tail_ref)

    u = gc_ref[...] * xt_ref[...]
    row = lax.broadcasted_iota(jnp.int32, u.shape, 0)
    acc = cw_ref[width - 1:width, :] * u
    for back in range(1, width):
        shifted = pltpu.roll(u, back, 0)
        for r in range(back):
            prev = tail_ref[hist - back + r:hist - back + r + 1, :]
            shifted = jnp.where(row == r, prev, shifted)
        acc = acc + cw_ref[width - 1 - back:width - back, :] * shifted
    o_ref[...] = (gb_ref[...] * acc).astype(o_ref.dtype)
    tail_ref[...] = u[ts - hist:, :]


def _short_conv(pa, conv_w, batch, seq):
    t = pa.shape[0]
    c = conv_w.shape[1]
    ts = _pick(seq, (256, 128))
    tc = _pick(c, (512, 256, 128))
    nc = c // tc
    ns = seq // ts
    return pl.pallas_call(
        _conv_kernel,
        out_shape=jax.ShapeDtypeStruct((t, c), BF16),
        grid=(batch * nc, ns),
        in_specs=[
            pl.BlockSpec((ts, tc), lambda bj, s: ((bj // nc) * ns + s, bj % nc)),
            pl.BlockSpec((ts, tc), lambda bj, s: ((bj // nc) * ns + s, nc + bj % nc)),
            pl.BlockSpec((ts, tc), lambda bj, s: ((bj // nc) * ns + s, 2 * nc + bj % nc)),
            pl.BlockSpec((conv_w.shape[0], tc), lambda bj, s: (0, bj % nc)),
        ],
        out_specs=pl.BlockSpec((ts, tc), lambda bj, s: ((bj // nc) * ns + s, bj % nc)),
        scratch_shapes=[pltpu.VMEM((8, tc), F32)],
        compiler_params=_params(("parallel", "arbitrary")),
        name="short_conv",
    )(pa, pa, pa, conv_w)


def _transposed_bf16(block):
    return block.astype(F32).T.astype(BF16)


def _init_softmax_state(m_ref, l_ref, acc_ref, slot):
    m_ref[slot] = jnp.full(m_ref.shape[1:], NEG_BIG, F32)
    l_ref[slot] = jnp.zeros(l_ref.shape[1:], F32)
    acc_ref[slot] = jnp.zeros(acc_ref.shape[1:], F32)


def _softmax_chunk(s, off, masked, v_t, m_ref, l_ref, acc_ref, slot):
    if masked:
        key = lax.broadcasted_iota(jnp.int32, s.shape, 0)
        qry = lax.broadcasted_iota(jnp.int32, s.shape, 1)
        s = jnp.where(qry >= key, s, NEG_BIG)
    m_prev = m_ref[slot]
    m_new = jnp.maximum(m_prev, jnp.max(s, axis=0, keepdims=True) + off)
    alpha = jnp.exp2(m_prev - m_new)
    p = jnp.exp2(s - (m_new - off))
    l_ref[slot] = alpha * l_ref[slot] + jnp.sum(p, axis=0, keepdims=True)
    acc_ref[slot] = alpha * acc_ref[slot] + jnp.dot(v_t, p.astype(BF16), preferred_element_type=F32)
    m_ref[slot] = m_new


def _causal_chunks(qi, tq, chunk):
    def body(j, carry):
        chunk(pl.multiple_of(j * (2 * tq), 2 * tq), 2 * tq, False)
        return carry

    lax.fori_loop(0, qi // 2, body, 0)

    @pl.when(qi % 2 == 1)
    def _():
        chunk(pl.multiple_of((qi - 1) * tq, 2 * tq), tq, False)

    chunk(pl.multiple_of(qi * tq, tq), tq, True)


def _diff_attn_kernel(lam_init, dh, slopes_ref, lam_ref, g_ref, q_ref, k_ref, v_ref, o_ref,
                      qt_ref, vt_ref, bias_ref, m_ref, l_ref, acc_ref):
    h = pl.program_id(1)
    qi = pl.program_id(2)
    tq = q_ref.shape[0]
    seq = k_ref.shape[0]

    slope2 = slopes_ref[h] * LOG2E

    @pl.when(qi == 0)
    def _():
        for c in range(seq // tq):
            rows = slice(c * tq, (c + 1) * tq)
            vt_ref[:, rows] = _transposed_bf16(v_ref[rows, :])
        bias_ref[...] = lax.broadcasted_iota(jnp.int32, bias_ref.shape, 0).astype(F32) * slope2

    for mp in range(2):
        qt_ref[mp] = _transposed_bf16(q_ref[:, mp * dh:(mp + 1) * dh])
        _init_softmax_state(m_ref, l_ref, acc_ref, mp)

    def chunk(start, n, masked):
        off = ((start - qi * tq) + jnp.zeros((1, tq), jnp.int32)).astype(F32) * slope2
        v_t = vt_ref[:, pl.ds(start, n)]
        for mp in range(2):
            k = k_ref[pl.ds(start, n), mp * dh:(mp + 1) * dh]
            s = jnp.dot(k, qt_ref[mp], preferred_element_type=F32) + bias_ref[0:n, :]
            _softmax_chunk(s, off, masked, v_t, m_ref, l_ref, acc_ref, mp)

    _causal_chunks(qi, tq, chunk)

    lp = lam_ref[...]
    lam = (jnp.exp(jnp.sum(lp[0:1] * lp[1:2], axis=-1, keepdims=True))
           - jnp.exp(jnp.sum(lp[2:3] * lp[3:4], axis=-1, keepdims=True)) + lam_init)
    o_t = acc_ref[0] * (1.0 / l_ref[0]) - lam * (acc_ref[1] * (1.0 / l_ref[1]))
    rms = lax.rsqrt(jnp.mean(o_t * o_t, axis=0, keepdims=True) + EPS)
    o_ref[...] = ((o_t * rms).T * g_ref[...] * (1.0 - lam_init)).astype(o_ref.dtype)


def _alibi_slopes(n):
    def pow2(m):
        start = 2.0 ** (-8.0 / m)
        return [start ** (i + 1) for i in range(m)]

    if math.log2(n).is_integer():
        s = pow2(n)
    else:
        c = 2 ** int(math.floor(math.log2(n)))
        s = pow2(c) + pow2(2 * c)[0::2][: n - c]
    return jnp.asarray(np.array(s, dtype=np.float32))


def _diff_attention(qkv, lam_p, subln_g, lam_init, batch, seq, heads, dh):
    t = qkv.shape[0]
    dv = 2 * dh
    tq = _pick(seq, (512, 256, 128))
    nq = seq // tq
    return pl.pallas_call(
        functools.partial(_diff_attn_kernel, lam_init, dh),
        out_shape=jax.ShapeDtypeStruct((t, heads * dv), BF16),
        grid=(batch, heads, nq),
        in_specs=[
            pl.BlockSpec(memory_space=pltpu.SMEM),
            pl.BlockSpec(lam_p.shape, lambda b, h, i: (0, 0)),
            pl.BlockSpec((1, dv), lambda b, h, i: (0, 0)),
            pl.BlockSpec((tq, dv), lambda b, h, i: (b * nq + i, h)),
            pl.BlockSpec((seq, dv), lambda b, h, i: (b, heads + h)),
            pl.BlockSpec((seq, dv), lambda b, h, i: (b, 2 * heads + h)),
        ],
        out_specs=pl.BlockSpec((tq, dv), lambda b, h, i: (b * nq + i, h)),
        scratch_shapes=[
            pltpu.VMEM((2, dh, tq), BF16),
            pltpu.VMEM((dv, seq), BF16),
            pltpu.VMEM((2 * tq, tq), F32),
            pltpu.VMEM((2, 1, tq), F32),
            pltpu.VMEM((2, 1, tq), F32),
            pltpu.VMEM((2, dv, tq), F32),
        ],
        compiler_params=_params(("parallel", "parallel", "arbitrary")),
        name="diff_attn",
    )(_alibi_slopes(heads), lam_p, subln_g, qkv, qkv, qkv)


def _fox_attn_kernel(dh, cum_ref, q_ref, k_ref, v_ref, o_ref,
                     qt_ref, vt_ref, rel_ref, first_ref, m_ref, l_ref, acc_ref):
    pair = pl.program_id(1)
    qi = pl.program_id(2)
    tq = q_ref.shape[0]
    seq = k_ref.shape[0]
    reps = tq // LANES

    span = 2 * tq

    @pl.when(qi == 0)
    def _():
        lane = lax.broadcasted_iota(jnp.int32, (span, LANES), 1)
        for c in range(seq // span):
            rows = slice(c * span, (c + 1) * span)
            cum = cum_ref[rows, :]
            for hh in range(2):
                vt_ref[hh, :, rows] = _transposed_bf16(v_ref[rows, hh * dh:(hh + 1) * dh])
                col = jnp.sum(jnp.where(lane == 2 * pair + hh, cum, 0.0), axis=1, keepdims=True)
                first = col[0:1, :]
                rel_ref[hh, rows, :] = jnp.broadcast_to((first - col) * LOG2E, (span, LANES))
                first_ref[hh, c:c + 1, :] = jnp.broadcast_to(first, (1, LANES))

    for hh in range(2):
        qt_ref[hh] = _transposed_bf16(q_ref[:, hh * dh:(hh + 1) * dh])
        _init_softmax_state(m_ref, l_ref, acc_ref, hh)

    def chunk(start, n, masked):
        for hh in range(2):
            k = k_ref[pl.ds(start, n), hh * dh:(hh + 1) * dh]
            rel = rel_ref[hh, pl.ds(start, n), :]
            s = jnp.dot(k, qt_ref[hh], preferred_element_type=F32) + jnp.concatenate([rel] * reps, axis=1)
            off = (first_ref[hh, pl.ds(qi // 2, 1), :] - first_ref[hh, pl.ds(start // span, 1), :]) * LOG2E
            off = jnp.concatenate([off] * reps, axis=1)
            _softmax_chunk(s, off, masked, vt_ref[hh, :, pl.ds(start, n)], m_ref, l_ref, acc_ref, hh)

    _causal_chunks(qi, tq, chunk)

    for hh in range(2):
        o_t = acc_ref[hh] * (1.0 / l_ref[hh])
        o_ref[:, hh * dh:(hh + 1) * dh] = o_t.T.astype(o_ref.dtype)


def _fox_attention(qkv, cum, batch, seq, heads, dh):
    t = qkv.shape[0]
    assert heads % 2 == 0
    pairs = heads // 2
    tq = _pick(seq, (512, 256, 128))
    nq = seq // tq
    assert nq % 2 == 0
    return pl.pallas_call(
        functools.partial(_fox_attn_kernel, dh),
        out_shape=jax.ShapeDtypeStruct((t, heads * dh), BF16),
        grid=(batch, pairs, nq),
        in_specs=[
            pl.BlockSpec((seq, LANES), lambda b, h, i: (b, 0)),
            pl.BlockSpec((tq, 2 * dh), lambda b, h, i: (b * nq + i, h)),
            pl.BlockSpec((seq, 2 * dh), lambda b, h, i: (b, pairs + h)),
            pl.BlockSpec((seq, 2 * dh), lambda b, h, i: (b, 2 * pairs + h)),
        ],
        out_specs=pl.BlockSpec((tq, 2 * dh), lambda b, h, i: (b * nq + i, h)),
        scratch_shapes=[
            pltpu.VMEM((2, dh, tq), BF16),
            pltpu.VMEM((2, dh, seq), BF16),
            pltpu.VMEM((2, seq, LANES), F32),
            pltpu.VMEM((2, nq, LANES), F32),
            pltpu.VMEM((2, 1, tq), F32),
            pltpu.VMEM((2, 1, tq), F32),
            pltpu.VMEM((2, dh, tq), F32),
        ],
        compiler_params=_params(("parallel", "parallel", "arbitrary")),
        name="fox_attn",
    )(cum, qkv, qkv, qkv)


def _forget_cumsum_kernel(f_ref, b_ref, o_ref):
    seq = f_ref.shape[0]
    ck = LANES
    r = lax.broadcasted_iota(jnp.int32, (ck, ck), 0)
    c = lax.broadcasted_iota(jnp.int32, (ck, ck), 1)
    tril = (r >= c).astype(F32)

    def body(i, carry):
        rows = pl.ds(pl.multiple_of(i * ck, ck), ck)
        z = f_ref[rows, :] + b_ref[...]
        log_f = jnp.minimum(z, 0.0) - jnp.log1p(jnp.exp(-jnp.abs(z)))
        cs = jnp.dot(tril, log_f, preferred_element_type=F32, precision=lax.Precision.HIGHEST) + carry
        o_ref[rows, :] = cs
        return cs[ck - 1:ck, :]

    lax.fori_loop(0, seq // ck, body, jnp.zeros((1, ck), F32))


def _forget_cumsum(f_proj, f_bias, batch, seq):
    return pl.pallas_call(
        _forget_cumsum_kernel,
        out_shape=jax.ShapeDtypeStruct(f_proj.shape, F32),
        grid=(batch,),
        in_specs=[
            pl.BlockSpec((seq, LANES), lambda b: (b, 0)),
            pl.BlockSpec((1, LANES), lambda b: (0, 0)),
        ],
        out_specs=pl.BlockSpec((seq, LANES), lambda b: (b, 0)),
        compiler_params=_params(("parallel",)),
        name="forget_cumsum",
    )(f_proj, f_bias)


def _sgu_kernel(u_ref, z_ref, w_ref, b_ref, lng_ref, lnb_ref, o_ref):
    rows, width = u_ref.shape
    groups, ck, _ = w_ref.shape
    gw = width // groups
    r = lax.broadcasted_iota(jnp.int32, (ck, ck), 0)
    c = lax.broadcasted_iota(jnp.int32, (ck, ck), 1)
    w_tri = [jnp.where(r >= c, w_ref[g], 0.0).astype(BF16) for g in range(groups)]
    for ci in range(rows // ck):
        rs = slice(ci * ck, (ci + 1) * ck)
        z = _gelu_tanh(z_ref[rs, :])
        mu = jnp.mean(z, axis=-1, keepdims=True)
        zc = z - mu
        var = jnp.mean(zc * zc, axis=-1, keepdims=True)
        zn = (zc * lax.rsqrt(var + EPS) * lng_ref[...] + lnb_ref[...]).astype(BF16)
        for g in range(groups):
            cs = slice(g * gw, (g + 1) * gw)
            mixed = jnp.dot(w_tri[g], zn[:, cs], preferred_element_type=F32) + b_ref[g]
            o_ref[rs, cs] = (_gelu_tanh(u_ref[rs, cs]) * mixed).astype(o_ref.dtype)


def _sgu(uz, sgu_w, sgu_b, ln_g, ln_b):
    t = uz.shape[0]
    groups, ck, _ = sgu_w.shape
    width = ln_g.shape[1]
    rows = _pick(t, (2 * ck, ck))
    return pl.pallas_call(
        _sgu_kernel,
        out_shape=jax.ShapeDtypeStruct((t, width), BF16),
        grid=(t // rows,),
        in_specs=[
            pl.BlockSpec((rows, width), lambda i: (i, 0)),
            pl.BlockSpec((rows, width), lambda i: (i, 1)),
            pl.BlockSpec(sgu_w.shape, lambda i: (0, 0, 0)),
            pl.BlockSpec(sgu_b.shape, lambda i: (0, 0, 0)),
            pl.BlockSpec((1, width), lambda i: (0, 0)),
            pl.BlockSpec((1, width), lambda i: (0, 0)),
        ],
        out_specs=pl.BlockSpec((rows, width), lambda i: (i, 0)),
        compiler_params=_params(("parallel",)),
        name="sgu",
    )(uz, uz, sgu_w, sgu_b, ln_g, ln_b)


def _query_col_scale(n, q_first, n_query, dh):
    col = jnp.arange(n)
    is_query = jnp.logical_and(col >= q_first, col < q_first + n_query)
    return jnp.where(is_query, dh ** -0.5 * LOG2E, 1.0).astype(F32)[None, :]


def _even_mixer(x, g_pre, g_post, w_in, w_out, conv_w, lam_p, subln_g, layer_idx, batch, seq):
    d = x.shape[1]
    conv_ch = conv_w.shape[1]
    dh = lam_p.shape[1]
    dv = subln_g.shape[0]
    heads = (d - conv_ch) // dv
    n_in = w_in.shape[1]
    n_attn = n_in - 3 * conv_ch
    pa, qkv = _in_proj(
        x, g_pre, w_in.astype(BF16), _query_col_scale(n_in, 3 * conv_ch, 2 * heads * dh, dh),
        ((3 * conv_ch, 3 * conv_ch, F32, False), (n_attn, n_attn, BF16, True)), "even_in")
    ya = _short_conv(pa, conv_w, batch, seq)
    lam_init = 0.8 - 0.6 * math.exp(-0.3 * layer_idx)
    yb = _diff_attention(qkv, lam_p, subln_g[None, :], lam_init, batch, seq, heads, dh)
    return _out_proj(ya, yb, w_out.astype(BF16), x, g_post)


def _odd_mixer(x, g_pre, g_post, w_in, w_out, sgu_w, sgu_b, ln_g, ln_b, f_bias, batch, seq):
    d = x.shape[1]
    width = ln_g.shape[0]
    heads = f_bias.shape[0]
    dh = (d - width) // heads
    n_attn = 3 * heads * dh
    assert heads <= LANES
    gate_cols = _pick(math.gcd(2 * width, n_attn), (512, 256, 128))
    n_in = 2 * width + n_attn + gate_cols
    w_bf = jnp.pad(w_in.astype(BF16), ((0, 0), (0, n_in - w_in.shape[1])))
    uz, qkv, f_proj = _in_proj(
        x, g_pre, w_bf, _query_col_scale(n_in, 2 * width, heads * dh, dh),
        ((2 * width, 2 * width, F32, False), (n_attn, n_attn, BF16, True), (gate_cols, LANES, F32, False)),
        "odd_in")
    bias =jnp.pad(f_bias, (0, LANES - heads))[None, :]
    cum = _forget_cumsum(f_proj, bias, batch, seq)
    yc = _sgu(uz, sgu_w, sgu_b[:, :, None], ln_g[None, :], ln_b[None, :])
    yd = _fox_attention(qkv, cum, batch, seq, heads, dh)
    return _out_proj(yc, yd, w_out.astype(BF16), x, g_post)


def kernel(x, norm_g, ffn_w_in, ffn_w_out, even_w_in, even_w_out, conv_w, diff_lambda, diff_subln_g,
           odd_w_in, odd_w_out, sgu_w, sgu_b, sgu_ln_g, sgu_ln_b, fox_f_bias):
    batch, seq, d = x.shape
    depth = norm_g.shape[0]
    xt = x.reshape(batch * seq, d)
    w_in, w_out = ffn_w_in[0, 0].astype(BF16), ffn_w_out[0, 0].astype(BF16)
    for l in range(depth):
        g = norm_g[l][:, None, :]
        xt, w_in, w_out = _ffn(xt, g[0], g[1], w_in, w_out, (ffn_w_in, ffn_w_out, l, 1))
        i = l // 2
        if l % 2 == 0:
            xt = _even_mixer(xt, g[2], g[3], even_w_in[i], even_w_out[i], conv_w[i], diff_lambda[i],
                             diff_subln_g[i], l, batch, seq)
        else:
            xt = _odd_mixer(xt, g[2], g[3], odd_w_in[i], odd_w_out[i], sgu_w[i], sgu_b[i],
                            sgu_ln_g[i], sgu_ln_b[i], fox_f_bias[i], batch, seq)
        if l + 1 < depth:
            xt, w_in, w_out = _ffn(xt, g[4], g[5], w_in, w_out, (ffn_w_in, ffn_w_out, l + 1, 0))
        else:
            (xt,) = _ffn(xt, g[4], g[5], w_in, w_out)
    return xt.reshape(batch, seq, d)
```

```python
import functools
import math

import jax
import jax.numpy as jnp
import numpy as np
from jax import lax
from jax.experimental import pallas as pl
from jax.experimental.pallas import tpu as pltpu

EPS = 1e-6
F32 = jnp.float32
BF16 = jnp.bfloat16
NEG_BIG = -1e30
V7X_VMEM_LIMIT_BYTES = 60000 * 1024
LANES = 128
LOG2E = math.log2(math.e)
NORM_ROWS = 128
NORM_SUB_ROWS = 16


def _params(semantics):
    return pltpu.CompilerParams(dimension_semantics=semantics, vmem_limit_bytes=V7X_VMEM_LIMIT_BYTES)


def _pick(n, cands):
    for c in cands:
        if n % c == 0:
            return c
    raise ValueError(f"no tile in {cands} divides {n}")


def _rms_scale(v):
    return lax.rsqrt(jnp.mean(v * v, axis=-1, keepdims=True) + EPS)


def _normalise_rows(n_rows, src_ref, apply):
    step = min(NORM_ROWS, n_rows)
    sub = min(NORM_SUB_ROWS, step)
    assert n_rows % step == 0 and step % sub == 0

    def it(c, carry):
        first = pl.multiple_of(c * step, step)
        scale = _rms_scale(src_ref[pl.ds(first, step), :])
        for k in range(step // sub):
            apply(pl.ds(first + k * sub, sub), scale[k * sub:(k + 1) * sub])
        return carry

    lax.fori_loop(0, n_rows // step, it, 0)


def _gelu_tanh(v):
    return 0.5 * v * (1.0 + jnp.tanh(math.sqrt(2.0 / math.pi) * (v + 0.044715 * (v * v * v))))


def _ffn_kernel(cast_next, x_ref, gpre_ref, gpost_ref, wg_ref, wu_ref, wo_ref, *refs):
    if cast_next:
        nin_ref, nout_ref, o_ref, nin_bf_ref, nout_bf_ref, h_ref = refs
    else:
        o_ref, h_ref = refs
    f = pl.program_id(1)
    tm = x_ref.shape[0]

    @pl.when(f == 0)
    def _():
        def norm_rows(rows, scale):
            h_ref[rows, :] = (x_ref[rows, :] * scale * gpre_ref[...]).astype(BF16)
            o_ref[rows, :] = jnp.zeros((rows.size, x_ref.shape[1]), F32)

        _normalise_rows(tm, x_ref, norm_rows)

    h = h_ref[...]
    gate = jnp.dot(h, wg_ref[...], preferred_element_type=F32)
    up = jnp.dot(h, wu_ref[...], preferred_element_type=F32)
    act = (gate / (1.0 + jnp.exp(-gate)) * up).astype(BF16)
    o_ref[...] += jnp.dot(act, wo_ref[...], preferred_element_type=F32)
    if cast_next:
        nin_bf_ref[...] = nin_ref[...].astype(BF16)
        nout_bf_ref[...] = nout_ref[...].astype(BF16)

    @pl.when(f == pl.num_programs(1) - 1)
    def _():
        def finish_rows(rows, scale):
            o_ref[rows, :] = x_ref[rows, :] + o_ref[rows, :] * (0.5 * scale) * gpost_ref[...]

        _normalise_rows(tm, o_ref, finish_rows)


def _ffn(x, g_pre, g_post, w_in, w_out, next_weights=None):
    t, d = x.shape
    d_ff = w_out.shape[0]
    tm = _pick(t, (512, 256, 128))
    tf = _pick(d_ff, (256, 128))
    nt, nf = t // tm, d_ff // tf
    in_specs = [
        pl.BlockSpec((tm, d), lambda i, f: (i, 0)),
        pl.BlockSpec((1, d), lambda i, f: (0, 0)),
        pl.BlockSpec((1, d), lambda i, f: (0, 0)),
        pl.BlockSpec((d, tf), lambda i, f: (0, f)),
        pl.BlockSpec((d, tf), lambda i, f: (0, f + nf)),
        pl.BlockSpec((tf, d), lambda i, f: (f, 0)),
    ]
    out_shape = [jax.ShapeDtypeStruct((t, d), F32)]
    out_specs = [pl.BlockSpec((tm, d), lambda i, f: (i, 0))]
    operands = [x, g_pre, g_post, w_in, w_in, w_out]
    if next_weights is not None:
        nw_in, nw_out, layer, half = next_weights
        assert d % nt == 0 and (d // nt) % LANES == 0 and (2 * tf) % LANES == 0
        in_blk, out_blk = (d // nt, 2 * tf), (tf, d // nt)
        in_specs += [
            pl.BlockSpec((None, None) + in_blk, lambda i, f: (layer, half, i, f)),
            pl.BlockSpec((None, None) + out_blk, lambda i, f: (layer, half, f, i)),
        ]
        out_shape += [jax.ShapeDtypeStruct(nw_in.shape[2:], BF16), jax.ShapeDtypeStruct(nw_out.shape[2:], BF16)]
        out_specs += [pl.BlockSpec(in_blk, lambda i, f: (i, f)), pl.BlockSpec(out_blk, lambda i, f: (f, i))]
        operands += [nw_in, nw_out]
    return pl.pallas_call(
        functools.partial(_ffn_kernel, next_weights is not None),
        out_shape=out_shape,
        grid=(nt, nf),
        in_specs=in_specs,
        out_specs=out_specs,
        scratch_shapes=[pltpu.VMEM((tm, d), BF16)],
        compiler_params=_params(("parallel", "arbitrary")),
        name="ffn",
    )(*operands)


def _in_proj_kernel(seg_steps, scaled, x_ref, g_ref, w_ref, cs_ref, *refs):
    out_refs, h_ref = refs[:-1], refs[-1]
    j = pl.program_id(1)

    @pl.when(j == 0)
    def _():
        def norm_rows(rows, scale):
            h_ref[rows, :] = (x_ref[rows, :] * scale * g_ref[...]).astype(BF16)

        _normalise_rows(x_ref.shape[0], x_ref, norm_rows)

    first = 0
    for steps, use_scale, o_ref in zip(seg_steps, scaled, out_refs):
        @pl.when(jnp.logical_and(j >= first, j < first + steps))
        def _(use_scale=use_scale, o_ref=o_ref):
            acc = jnp.dot(h_ref[...], w_ref[...], preferred_element_type=F32)
            if use_scale:
                acc = acc * cs_ref[...]
            o_ref[...] = acc[:, :o_ref.shape[1]].astype(o_ref.dtype)

        first += steps


def _in_proj(x, g, w, col_scale, segments, name):
    t, d = x.shape
    tm = _pick(t, (1024, 512, 256, 128))
    tn = _pick(functools.reduce(math.gcd, [s[0] for s in segments]), (512, 256, 128))
    seg_steps = tuple(s[0] // tn for s in segments)
    assert sum(s[0] for s in segments) == w.shape[1]
    firsts = [sum(seg_steps[:k]) for k in range(len(segments))]
    out_shapes, out_specs = [], []
    for (n_w, n_out, dtype, _), first, steps in zip(segments, firsts, seg_steps):
        assert n_out == n_w or (steps == 1 and n_out <= tn)
        bn = tn if n_out == n_w else n_out
        out_shapes.append(jax.ShapeDtypeStruct((t, n_out), dtype))
        out_specs.append(pl.BlockSpec(
            (tm, bn), lambda i, j, first=first, steps=steps: (i, jnp.clip(j - first, 0, steps - 1))))
    return pl.pallas_call(
        functools.partial(_in_proj_kernel, seg_steps, tuple(s[3] for s in segments)),
        out_shape=out_shapes,
        grid=(t // tm, sum(seg_steps)),
        in_specs=[
            pl.BlockSpec((tm, d), lambda i, j: (i, 0)),
            pl.BlockSpec((1, d), lambda i, j: (0, 0)),
            pl.BlockSpec((d, tn), lambda i, j: (0, j)),
            pl.BlockSpec((1, tn), lambda i, j: (0, j)),
        ],
        out_specs=out_specs,
        scratch_shapes=[pltpu.VMEM((tm, d), BF16)],
        compiler_params=_params(("parallel", "arbitrary")),
        name=name,
    )(x, g, w, col_scale)


def _out_proj_kernel(na, ya_ref, yb_ref, w_ref, x_ref, g_ref, o_ref):
    k = pl.program_id(1)
    tm = x_ref.shape[0]

    @pl.when(k == 0)
    def _():
        o_ref[...] = jnp.dot(ya_ref[...], w_ref[...], preferred_element_type=F32)

    @pl.when(jnp.logical_and(k > 0, k < na))
    def _():
        o_ref[...] += jnp.dot(ya_ref[...], w_ref[...], preferred_element_type=F32)

    @pl.when(k >= na)
    def _():
        o_ref[...] += jnp.dot(yb_ref[...], w_ref[...], preferred_element_type=F32)

    @pl.when(k == pl.num_programs(1) - 1)
    def _():
        def finish_rows(rows, scale):
            o_ref[rows, :] = x_ref[rows, :] + o_ref[rows, :] * scale * g_ref[...]

        _normalise_rows(tm, o_ref, finish_rows)


def _out_proj(ya, yb, w, x, g):
    t, d = x.shape
    wa, wb = ya.shape[1], yb.shape[1]
    tm = _pick(t, (512, 256, 128))
    tk = _pick(math.gcd(wa, wb), (1024, 512, 256, 128))
    na, nb = wa // tk, wb // tk
    return pl.pallas_call(
        functools.partial(_out_proj_kernel, na),
        out_shape=jax.ShapeDtypeStruct((t, d), F32),
        grid=(t // tm, na + nb),
        in_specs=[
            pl.BlockSpec((tm, tk), lambda i, k: (i, jnp.minimum(k, na - 1))),
            pl.BlockSpec((tm, tk), lambda i, k: (i, jnp.maximum(k - na, 0))),
            pl.BlockSpec((tk, d), lambda i, k: (k, 0)),
            pl.BlockSpec((tm, d), lambda i, k: (i, 0)),
            pl.BlockSpec((1, d), lambda i, k: (0, 0)),
        ],
        out_specs=pl.BlockSpec((tm, d), lambda i, k: (i, 0)),
        compiler_params=_params(("parallel", "arbitrary")),
        name="out_proj",
    )(ya, yb, w, x, g)


def _conv_kernel(gb_ref, gc_ref, xt_ref, cw_ref, o_ref, tail_ref):
    ts = gb_ref.shape[0]
    width = cw_ref.shape[0]
    hist = tail_ref.shape[0]

    @pl.when(pl.program_id(1) == 0)
    def _():
        tail_ref[...] = jnp.zeros_like(tail_ref)

    u = gc_ref[...] * xt_ref[...]
    row = lax.broadcasted_iota(jnp.int32, u.shape, 0)
    acc = cw_ref[width - 1:width, :] * u
    for back in range(1, width):
        shifted = pltpu.roll(u, back, 0)
        for r in range(back):
            prev = tail_ref[hist - back + r:hist - back + r + 1, :]
            shifted = jnp.where(row == r, prev, shifted)
        acc = acc + cw_ref[width - 1 - back:width - back, :] * shifted
    o_ref[...] = (gb_ref[...] * acc).astype(o_ref.dtype)
    tail_ref[...] = u[ts - hist:, :]


def _short_conv(pa, conv_w, batch, seq):
    t = pa.shape[0]
    c = conv_w.shape[1]
    ts = _pick(seq, (256, 128))
    tc = _pick(c, (512, 256, 128))
    nc = c // tc
    ns = seq // ts
    return pl.pallas_call(
        _conv_kernel,
        out_shape=jax.ShapeDtypeStruct((t, c), BF16),
        grid=(batch * nc, ns),
        in_specs=[
            pl.BlockSpec((ts, tc), lambda bj, s: ((bj // nc) * ns + s, bj % nc)),
            pl.BlockSpec((ts, tc), lambda bj, s: ((bj // nc) * ns + s, nc + bj % nc)),
            pl.BlockSpec((ts, tc), lambda bj, s: ((bj // nc) * ns + s, 2 * nc + bj % nc)),
            pl.BlockSpec((conv_w.shape[0], tc), lambda bj, s: (0, bj % nc)),
        ],
        out_specs=pl.BlockSpec((ts, tc), lambda bj, s: ((bj // nc) * ns + s, bj % nc)),
        scratch_shapes=[pltpu.VMEM((8, tc), F32)],
        compiler_params=_params(("parallel", "arbitrary")),
        name="short_conv",
    )(pa, pa, pa, conv_w)


def _transposed_bf16(block):
    return block.astype(F32).T.astype(BF16)


def _init_softmax_state(m_ref, l_ref, acc_ref, slot):
    m_ref[slot] = jnp.full(m_ref.shape[1:], NEG_BIG, F32)
    l_ref[slot] = jnp.zeros(l_ref.shape[1:], F32)
    acc_ref[slot] = jnp.zeros(acc_ref.shape[1:], F32)


def _softmax_chunk(s, off, masked, v_t, m_ref, l_ref, acc_ref, slot):
    if masked:
        key = lax.broadcasted_iota(jnp.int32, s.shape, 0)
        qry = lax.broadcasted_iota(jnp.int32, s.shape, 1)
        s = jnp.where(qry >= key, s, NEG_BIG)
    m_prev = m_ref[slot]
    m_new = jnp.maximum(m_prev, jnp.max(s, axis=0, keepdims=True) + off)
    alpha = jnp.exp2(m_prev - m_new)
    p = jnp.exp2(s - (m_new - off))
    l_ref[slot] = alpha * l_ref[slot] + jnp.sum(p, axis=0, keepdims=True)
    acc_ref[slot] = alpha * acc_ref[slot] + jnp.dot(v_t, p.astype(BF16), preferred_element_type=F32)
    m_ref[slot] = m_new


def _pipelined_causal_chunks(qi, logits, update, s0_ref, s1_ref):
    logits(0, s0_ref)

    def body(p, carry):
        c = 2 * p
        logits(c + 1, s1_ref)
        update(c, s0_ref, False)
        logits(c + 2, s0_ref)
        update(c + 1, s1_ref, False)
        return carry

    lax.fori_loop(0, qi // 2, body, 0)

    @pl.when(qi % 2 == 0)
    def _():
        update(qi, s0_ref, True)

    @pl.when(qi % 2 == 1)
    def _():
        logits(qi, s1_ref)
        update(qi - 1, s0_ref, False)
        update(qi, s1_ref, True)


def _diff_attn_kernel(lam_init, dh, slopes_ref, lam_ref, g_ref, q_ref, k_ref, v_ref, o_ref,
                      qt_ref, vt_ref, bias_ref, s0_ref, s1_ref, m_ref, l_ref, acc_ref):
    h = pl.program_id(1)
    qi = pl.program_id(2)
    tq = q_ref.shape[0]
    seq = k_ref.shape[0]

    slope2 = slopes_ref[h] * LOG2E

    @pl.when(qi == 0)
    def _():
        for c in range(seq // tq):
            rows = slice(c * tq, (c + 1) * tq)
            vt_ref[:, rows] = _transposed_bf16(v_ref[rows, :])
        bias_ref[...] = lax.broadcasted_iota(jnp.int32, bias_ref.shape, 0).astype(F32) * slope2

    for mp in range(2):
        qt_ref[mp] = _transposed_bf16(q_ref[:, mp * dh:(mp + 1) * dh])
        _init_softmax_state(m_ref, l_ref, acc_ref, mp)

    def logits(c, s_ref):
        rows = pl.ds(pl.multiple_of(c * tq, tq), tq)
        for mp in range(2):
            k = k_ref[rows, mp * dh:(mp + 1) * dh]
            s_ref[mp] = jnp.dot(k, qt_ref[mp], preferred_element_type=F32) + bias_ref[...]

    def update(c, s_ref, masked):
        off = ((c - qi) * tq + jnp.zeros((1, tq), jnp.int32)).astype(F32) * slope2
        v_t = vt_ref[:, pl.ds(pl.multiple_of(c * tq, tq), tq)]
        for mp in range(2):
            _softmax_chunk(s_ref[mp], off, masked, v_t, m_ref, l_ref, acc_ref, mp)

    _pipelined_causal_chunks(qi, logits, update, s0_ref, s1_ref)

    lp = lam_ref[...]
    lam = (jnp.exp(jnp.sum(lp[0:1] * lp[1:2], axis=-1, keepdims=True))
           - jnp.exp(jnp.sum(lp[2:3] * lp[3:4], axis=-1, keepdims=True)) + lam_init)
    o_t = acc_ref[0] * (1.0 / l_ref[0]) - lam * (acc_ref[1] * (1.0 / l_ref[1]))
    rms = lax.rsqrt(jnp.mean(o_t * o_t, axis=0, keepdims=True) + EPS)
    o_ref[...] = ((o_t * rms).T * g_ref[...] * (1.0 - lam_init)).astype(o_ref.dtype)


def _alibi_slopes(n):
    def pow2(m):
        start = 2.0 ** (-8.0 / m)
        return [start ** (i + 1) for i in range(m)]

    if math.log2(n).is_integer():
        s = pow2(n)
    else:
        c = 2 ** int(math.floor(math.log2(n)))
        s = pow2(c) + pow2(2 * c)[0::2][: n - c]
    return jnp.asarray(np.array(s, dtype=np.float32))


def _diff_attention(qkv, lam_p, subln_g, lam_init, batch, seq, heads, dh):
    t = qkv.shape[0]
    dv = 2 * dh
    tq = _pick(seq, (512, 256, 128))
    nq = seq // tq
    return pl.pallas_call(
        functools.partial(_diff_attn_kernel, lam_init, dh),
        out_shape=jax.ShapeDtypeStruct((t, heads * dv), BF16),
        grid=(batch, heads, nq),
        in_specs=[
            pl.BlockSpec(memory_space=pltpu.SMEM),
            pl.BlockSpec(lam_p.shape, lambda b, h, i: (0, 0)),
            pl.BlockSpec((1, dv), lambda b, h, i: (0, 0)),
            pl.BlockSpec((tq, dv), lambda b, h, i: (b * nq + i, h)),
            pl.BlockSpec((seq, dv), lambda b, h, i: (b, heads + h)),
            pl.BlockSpec((seq, dv), lambda b, h, i: (b, 2 * heads + h)),
        ],
        out_specs=pl.BlockSpec((tq, dv), lambda b, h, i: (b * nq + i, h)),
        scratch_shapes=[
            pltpu.VMEM((2, dh, tq), BF16),
            pltpu.VMEM((dv, seq), BF16),
            pltpu.VMEM((tq, tq), F32),
            pltpu.VMEM((2, tq, tq), F32),
            pltpu.VMEM((2, tq, tq), F32),
            pltpu.VMEM((2, 1, tq), F32),
            pltpu.VMEM((2, 1, tq), F32),
            pltpu.VMEM((2, dv, tq), F32),
        ],
        compiler_params=_params(("parallel", "parallel", "arbitrary")),
        name="diff_attn",
    )(_alibi_slopes(heads), lam_p, subln_g, qkv, qkv, qkv)


def _fox_attn_kernel(dh, cum_ref, q_ref, k_ref, v_ref, o_ref,
                     qt_ref, vt_ref, rel_ref, first_ref, s0_ref, s1_ref, m_ref, l_ref, acc_ref):
    pair = pl.program_id(1)
    qi = pl.program_id(2)
    tq = q_ref.shape[0]
    seq = k_ref.shape[0]
    reps = tq // LANES

    @pl.when(qi == 0)
    def _():
        lane = lax.broadcasted_iota(jnp.int32, (tq, LANES), 1)
        for c in range(seq // tq):
            rows = slice(c * tq, (c + 1) * tq)
            cum = cum_ref[rows, :]
            for hh in range(2):
                vt_ref[hh, :, rows] = _transposed_bf16(v_ref[rows, hh * dh:(hh + 1) * dh])
                col = jnp.sum(jnp.where(lane == 2 * pair + hh, cum, 0.0), axis=1, keepdims=True)
                first = col[0:1, :]
                rel_ref[hh, rows, :] = jnp.broadcast_to((first - col) * LOG2E, (tq, LANES))
                first_ref[hh, c:c + 1, :] = jnp.broadcast_to(first, (1, LANES))

    for hh in range(2):
        qt_ref[hh] = _transposed_bf16(q_ref[:, hh * dh:(hh + 1) * dh])
        _init_softmax_state(m_ref, l_ref, acc_ref, hh)

    def logits(c, s_ref):
        rows = pl.ds(pl.multiple_of(c * tq, tq), tq)
        for hh in range(2):
            k = k_ref[rows, hh * dh:(hh + 1) * dh]
            rel = rel_ref[hh, rows, :]
            s_ref[hh] = jnp.dot(k, qt_ref[hh], preferred_element_type=F32) + jnp.concatenate([rel] * reps, axis=1)

    def update(c, s_ref, masked):
        cols = pl.ds(pl.multiple_of(c * tq, tq), tq)
        for hh in range(2):
            off = (first_ref[hh, pl.ds(qi, 1), :] - first_ref[hh, pl.ds(c, 1), :]) * LOG2E
            off = jnp.concatenate([off] * reps, axis=1)
            _softmax_chunk(s_ref[hh], off, masked, vt_ref[hh, :, cols], m_ref, l_ref, acc_ref, hh)

    _pipelined_causal_chunks(qi, logits, update, s0_ref, s1_ref)

    for hh in range(2):
        o_t = acc_ref[hh] * (1.0 / l_ref[hh])
        o_ref[:, hh * dh:(hh + 1) * dh] = o_t.T.astype(o_ref.dtype)


def _fox_attention(qkv, cum, batch, seq, heads, dh):
    t = qkv.shape[0]
    assert heads % 2 == 0
    pairs = heads // 2
    tq = _pick(seq, (512, 256, 128))
    nq = seq // tq
    return pl.pallas_call(
        functools.partial(_fox_attn_kernel, dh),
        out_shape=jax.ShapeDtypeStruct((t, heads * dh), BF16),
        grid=(batch, pairs, nq),
        in_specs=[
            pl.BlockSpec((seq, LANES), lambda b, h, i: (b, 0)),
            pl.BlockSpec((tq, 2 * dh), lambda b, h, i: (b * nq + i, h)),
            pl.BlockSpec((seq, 2 * dh), lambda b, h, i: (b, pairs + h)),
            pl.BlockSpec((seq, 2 * dh), lambda b, h, i: (b, 2 * pairs + h)),
        ],
        out_specs=pl.BlockSpec((tq, 2 * dh), lambda b, h, i: (b * nq + i, h)),
        scratch_shapes=[
            pltpu.VMEM((2, dh, tq), BF16),
            pltpu.VMEM((2, dh, seq), BF16),
            pltpu.VMEM((2, seq, LANES), F32),
            pltpu.VMEM((2, nq, LANES), F32),
            pltpu.VMEM((2, tq, tq), F32),
            pltpu.VMEM((2, tq, tq), F32),
            pltpu.VMEM((2, 1, tq), F32),
            pltpu.VMEM((2, 1, tq), F32),
            pltpu.VMEM((2, dh, tq), F32),
        ],
        compiler_params=_params(("parallel", "parallel", "arbitrary")),
        name="fox_attn",
    )(cum, qkv, qkv, qkv)


def _forget_cumsum_kernel(f_ref, b_ref, o_ref):
    seq = f_ref.shape[0]
    ck = LANES
    r = lax.broadcasted_iota(jnp.int32, (ck, ck), 0)
    c = lax.broadcasted_iota(jnp.int32, (ck, ck), 1)
    tril = (r >= c).astype(F32)

    def body(i, carry):
        rows = pl.ds(pl.multiple_of(i * ck, ck), ck)
        z = f_ref[rows, :] + b_ref[...]
        log_f = jnp.minimum(z, 0.0) - jnp.log1p(jnp.exp(-jnp.abs(z)))
        cs = jnp.dot(tril, log_f, preferred_element_type=F32, precision=lax.Precision.HIGHEST) + carry
        o_ref[rows, :] = cs
        return cs[ck - 1:ck, :]

    lax.fori_loop(0, seq // ck, body, jnp.zeros((1, ck), F32))


def _forget_cumsum(f_proj, f_bias, batch, seq):
    return pl.pallas_call(
        _forget_cumsum_kernel,
        out_shape=jax.ShapeDtypeStruct(f_proj.shape, F32),
        grid=(batch,),
        in_specs=[
            pl.BlockSpec((seq, LANES), lambda b: (b, 0)),
            pl.BlockSpec((1, LANES), lambda b: (0, 0)),
        ],
        out_specs=pl.BlockSpec((seq, LANES), lambda b: (b, 0)),
        compiler_params=_params(("parallel",)),
        name="forget_cumsum",
    )(f_proj, f_bias)


def _sgu_kernel(u_ref, z_ref, w_ref, b_ref, lng_ref, lnb_ref, o_ref):
    rows, width = u_ref.shape
    groups, ck, _ = w_ref.shape
    gw = width // groups
    r = lax.broadcasted_iota(jnp.int32, (ck, ck), 0)
    c = lax.broadcasted_iota(jnp.int32, (ck, ck), 1)
    w_tri = [jnp.where(r >= c, w_ref[g], 0.0).astype(BF16) for g in range(groups)]
    for ci in range(rows // ck):
        rs = slice(ci * ck, (ci + 1) * ck)
        z = _gelu_tanh(z_ref[rs, :])
        mu = jnp.mean(z, axis=-1, keepdims=True)
        zc = z - mu
        var = jnp.mean(zc * zc, axis=-1, keepdims=True)
        zn = (zc * lax.rsqrt(var + EPS) * lng_ref[...] + lnb_ref[...]).astype(BF16)
        for g in range(groups):
            cs = slice(g * gw, (g + 1) * gw)
            mixed = jnp.dot(w_tri[g], zn[:, cs], preferred_element_type=F32) + b_ref[g]
            o_ref[rs, cs] = (_gelu_tanh(u_ref[rs, cs]) * mixed).astype(o_ref.dtype)


def _sgu(uz, sgu_w, sgu_b, ln_g, ln_b):
    t = uz.shape[0]
    groups, ck, _ = sgu_w.shape
    width = ln_g.shape[1]
    rows = _pick(t, (2 * ck, ck))
    return pl.pallas_call(
        _sgu_kernel,
        out_shape=jax.ShapeDtypeStruct((t, width), BF16),
        grid=(t // rows,),
        in_specs=[
            pl.BlockSpec((rows, width), lambda i: (i, 0)),
            pl.BlockSpec((rows, width), lambda i: (i, 1)),
            pl.BlockSpec(sgu_w.shape, lambda i: (0, 0, 0)),
            pl.BlockSpec(sgu_b.shape, lambda i: (0, 0, 0)),
            pl.BlockSpec((1, width), lambda i: (0, 0)),
            pl.BlockSpec((1, width), lambda i: (0, 0)),
        ],
        out_specs=pl.BlockSpec((rows, width), lambda i: (i, 0)),
        compiler_params=_params(("parallel",)),
        name="sgu",
    )(uz, uz, sgu_w, sgu_b, ln_g, ln_b)


def _query_col_scale(n, q_first, n_query, dh):
    col = jnp.arange(n)
    is_query = jnp.logical_and(col >= q_first, col < q_first + n_query)
    return jnp.where(is_query, dh ** -0.5 * LOG2E, 1.0).astype(F32)[None, :]


def _even_mixer(x, g_pre, g_post, w_in, w_out, conv_w, lam_p, subln_g, layer_idx, batch, seq):
    d = x.shape[1]
    conv_ch = conv_w.shape[1]
    dh = lam_p.shape[1]
    dv = subln_g.shape[0]
    heads = (d - conv_ch) // dv
    n_in = w_in.shape[1]
    n_attn = n_in - 3 * conv_ch
    pa, qkv = _in_proj(
        x, g_pre, w_in.astype(BF16), _query_col_scale(n_in, 3 * conv_ch, 2 * heads * dh, dh),
        ((3 * conv_ch, 3 * conv_ch, F32, False), (n_attn, n_attn, BF16, True)), "even_in")
    ya = _short_conv(pa, conv_w, batch, seq)
    lam_init = 0.8 - 0.6 * math.exp(-0.3 * layer_idx)
    yb = _diff_attention(qkv, lam_p, subln_g[None, :], lam_init, batch, seq, heads, dh)
    return _out_proj(ya, yb, w_out.astype(BF16), x, g_post)


def _odd_mixer(x, g_pre, g_post, w_in, w_out, sgu_w, sgu_b, ln_g, ln_b, f_bias, batch, seq):
    d = x.shape[1]
    width = ln_g.shape[0]
    heads = f_bias.shape[0]
    dh = (d - width) // heads
    n_attn = 3 * heads * dh
    assert heads <= LANES
    gate_cols = _pick(math.gcd(2 * width, n_attn), (512, 256, 128))
    n_in = 2 * width + n_attn + gate_cols
    w_bf = jnp.pad(w_in.astype(BF16), ((0, 0), (0, n_in - w_in.shape[1])))
    uz, qkv, f_proj = _in_proj(
        x, g_pre, w_bf, _query_col_scale(n_in, 2 * width, heads * dh, dh),
        ((2 * width, 2 * width, F32, False), (n_attn, n_attn, BF16, True), (gate_cols, LANES, F32, False)),
        "odd_in")
    bias =jnp.pad(f_bias, (0, LANES - heads))[None, :]
    cum = _forget_cumsum(f_proj, bias, batch, seq)
    yc = _sgu(uz, sgu_w, sgu_b[:, :, None], ln_g[None, :], ln_b[None, :])
    yd = _fox_attention(qkv, cum, batch, seq, heads, dh)
    return _out_proj(yc, yd, w_out.astype(BF16), x, g_post)


def kernel(x, norm_g, ffn_w_in, ffn_w_out, even_w_in, even_w_out, conv_w, diff_lambda, diff_subln_g,
           odd_w_in, odd_w_out, sgu_w, sgu_b, sgu_ln_g, sgu_ln_b, fox_f_bias):
    batch, seq, d = x.shape
    depth = norm_g.shape[0]
    xt = x.reshape(batch * seq, d)
    w_in, w_out = ffn_w_in[0, 0].astype(BF16), ffn_w_out[0, 0].astype(BF16)
    for l in range(depth):
        g = norm_g[l][:, None, :]
        xt, w_in, w_out = _ffn(xt, g[0], g[1], w_in, w_out, (ffn_w_in, ffn_w_out, l, 1))
        i = l // 2
        if l % 2 == 0:
            xt = _even_mixer(xt, g[2], g[3], even_w_in[i], even_w_out[i], conv_w[i], diff_lambda[i],
                             diff_subln_g[i], l, batch, seq)
        else:
            xt = _odd_mixer(xt, g[2], g[3], odd_w_in[i], odd_w_out[i], sgu_w[i], sgu_b[i],
                            sgu_ln_g[i], sgu_ln_b[i], fox_f_bias[i], batch, seq)
        if l + 1 < depth:
            xt, w_in, w_out = _ffn(xt, g[4], g[5], w_in, w_out, (ffn_w_in, ffn_w_out, l + 1, 0))
        else:
            (xt,) = _ffn(xt, g[4], g[5], w_in, w_out)
    return xt.reshape(batch, seq, d)
```

```python
import functools
import math

import jax
import jax.numpy as jnp
import numpy as np
from jax import lax
from jax.experimental import pallas as pl
from jax.experimental.pallas import tpu as pltpu

EPS = 1e-6
F32 = jnp.float32
BF16 = jnp.bfloat16
NEG_BIG = -1e30
V7X_VMEM_LIMIT_BYTES = 60000 * 1024
LANES = 128
LOG2E = math.log2(math.e)
BIAS_PIECES = 3
NORM_ROWS = 128
NORM_SUB_ROWS = 16


def _params(semantics):
    return pltpu.CompilerParams(dimension_semantics=semantics, vmem_limit_bytes=V7X_VMEM_LIMIT_BYTES)


def _pick(n, cands):
    for c in cands:
        if n % c == 0:
            return c
    raise ValueError(f"no tile in {cands} divides {n}")


def _rms_scale(v):
    return lax.rsqrt(jnp.mean(v * v, axis=-1, keepdims=True) + EPS)


def _normalise_rows(n_rows, src_ref, apply):
    step = min(NORM_ROWS, n_rows)
    sub = min(NORM_SUB_ROWS, step)
    assert n_rows % step == 0 and step % sub == 0

    def it(c, carry):
        first = pl.multiple_of(c * step, step)
        scale = _rms_scale(src_ref[pl.ds(first, step), :])
        for k in range(step // sub):
            apply(pl.ds(first + k * sub, sub), scale[k * sub:(k + 1) * sub])
        return carry

    lax.fori_loop(0, n_rows // step, it, 0)


def _gelu_tanh(v):
    return 0.5 * v * (1.0 + jnp.tanh(math.sqrt(2.0 / math.pi) * (v + 0.044715 * (v * v * v))))


def _ffn_kernel(cast_next, x_ref, gpre_ref, gpost_ref, wg_ref, wu_ref, wo_ref, *refs):
    if cast_next:
        nin_ref, nout_ref, o_ref, nin_bf_ref, nout_bf_ref, h_ref = refs
    else:
        o_ref, h_ref = refs
    f = pl.program_id(1)
    tm = x_ref.shape[0]

    @pl.when(f == 0)
    def _():
        def norm_rows(rows, scale):
            h_ref[rows, :] = (x_ref[rows, :] * scale * gpre_ref[...]).astype(BF16)
            o_ref[rows, :] = jnp.zeros((rows.size, x_ref.shape[1]), F32)

        _normalise_rows(tm, x_ref, norm_rows)

    h = h_ref[...]
    gate = jnp.dot(h, wg_ref[...], preferred_element_type=F32)
    up = jnp.dot(h, wu_ref[...], preferred_element_type=F32)
    act = (gate / (1.0 + jnp.exp(-gate)) * up).astype(BF16)
    o_ref[...] += jnp.dot(act, wo_ref[...], preferred_element_type=F32)
    if cast_next:
        nin_bf_ref[...] = nin_ref[...].astype(BF16)
        nout_bf_ref[...] = nout_ref[...].astype(BF16)

    @pl.when(f == pl.num_programs(1) - 1)
    def _():
        def finish_rows(rows, scale):
            o_ref[rows, :] = x_ref[rows, :] + o_ref[rows, :] * (0.5 * scale) * gpost_ref[...]

        _normalise_rows(tm, o_ref, finish_rows)


def _ffn(x, g_pre, g_post, w_in, w_out, next_weights=None):
    t, d = x.shape
    d_ff = w_out.shape[0]
    tm = _pick(t, (512, 256, 128))
    tf = _pick(d_ff, (256, 128))
    nt, nf = t // tm, d_ff // tf
    in_specs = [
        pl.BlockSpec((tm, d), lambda i, f: (i, 0)),
        pl.BlockSpec((1, d), lambda i, f: (0, 0)),
        pl.BlockSpec((1, d), lambda i, f: (0, 0)),
        pl.BlockSpec((d, tf), lambda i, f: (0, f)),
        pl.BlockSpec((d, tf), lambda i, f: (0, f + nf)),
        pl.BlockSpec((tf, d), lambda i, f: (f, 0)),
    ]
    out_shape = [jax.ShapeDtypeStruct((t, d), F32)]
    out_specs = [pl.BlockSpec((tm, d), lambda i, f: (i, 0))]
    operands = [x, g_pre, g_post, w_in, w_in, w_out]
    if next_weights is not None:
        nw_in, nw_out, layer, half = next_weights
        assert d % nt == 0 and (d // nt) % LANES == 0 and (2 * tf) % LANES == 0
        in_blk, out_blk = (d // nt, 2 * tf), (tf, d // nt)
        in_specs += [
            pl.BlockSpec((None, None) + in_blk, lambda i, f: (layer, half, i, f)),
            pl.BlockSpec((None, None) + out_blk, lambda i, f: (layer, half, f, i)),
        ]
        out_shape += [jax.ShapeDtypeStruct(nw_in.shape[2:], BF16), jax.ShapeDtypeStruct(nw_out.shape[2:], BF16)]
        out_specs += [pl.BlockSpec(in_blk, lambda i, f: (i, f)), pl.BlockSpec(out_blk, lambda i, f: (f, i))]
        operands += [nw_in, nw_out]
    return pl.pallas_call(
        functools.partial(_ffn_kernel, next_weights is not None),
        out_shape=out_shape,
        grid=(nt, nf),
        in_specs=in_specs,
        out_specs=out_specs,
        scratch_shapes=[pltpu.VMEM((tm, d), BF16)],
        compiler_params=_params(("parallel", "arbitrary")),
        name="ffn",
    )(*operands)


def _in_proj_kernel(seg_steps, scaled, x_ref, g_ref, w_ref, cs_ref, *refs):
    out_refs, h_ref = refs[:-1], refs[-1]
    j = pl.program_id(1)

    @pl.when(j == 0)
    def _():
        def norm_rows(rows, scale):
            h_ref[rows, :] = (x_ref[rows, :] * scale * g_ref[...]).astype(BF16)

        _normalise_rows(x_ref.shape[0], x_ref, norm_rows)

    first = 0
    for steps, use_scale, o_ref in zip(seg_steps, scaled, out_refs):
        @pl.when(jnp.logical_and(j >= first, j < first + steps))
        def _(use_scale=use_scale, o_ref=o_ref):
            acc = jnp.dot(h_ref[...], w_ref[...], preferred_element_type=F32)
            if use_scale:
                acc = acc * cs_ref[...]
            o_ref[...] = acc[:, :o_ref.shape[1]].astype(o_ref.dtype)

        first += steps


def _in_proj(x, g, w, col_scale, segments, name):
    t, d = x.shape
    tm = _pick(t, (1024, 512, 256, 128))
    tn = _pick(functools.reduce(math.gcd, [s[0] for s in segments]), (512, 256, 128))
    seg_steps = tuple(s[0] // tn for s in segments)
    assert sum(s[0] for s in segments) == w.shape[1]
    firsts = [sum(seg_steps[:k]) for k in range(len(segments))]
    out_shapes, out_specs = [], []
    for (n_w, n_out, dtype, _), first, steps in zip(segments, firsts, seg_steps):
        assert n_out == n_w or (steps == 1 and n_out <= tn)
        bn = tn if n_out == n_w else n_out
        out_shapes.append(jax.ShapeDtypeStruct((t, n_out), dtype))
        out_specs.append(pl.BlockSpec(
            (tm, bn), lambda i, j, first=first, steps=steps: (i, jnp.clip(j - first, 0, steps - 1))))
    return pl.pallas_call(
        functools.partial(_in_proj_kernel, seg_steps, tuple(s[3] for s in segments)),
        out_shape=out_shapes,
        grid=(t // tm, sum(seg_steps)),
        in_specs=[
            pl.BlockSpec((tm, d), lambda i, j: (i, 0)),
            pl.BlockSpec((1, d), lambda i, j: (0, 0)),
            pl.BlockSpec((d, tn), lambda i, j: (0, j)),
            pl.BlockSpec((1, tn), lambda i, j: (0, j)),
        ],
        out_specs=out_specs,
        scratch_shapes=[pltpu.VMEM((tm, d), BF16)],
        compiler_params=_params(("parallel", "arbitrary")),
        name=name,
    )(x, g, w, col_scale)


def _out_proj_kernel(na, ya_ref, yb_ref, w_ref, x_ref, g_ref, o_ref):
    k = pl.program_id(1)
    tm = x_ref.shape[0]

    @pl.when(k == 0)
    def _():
        o_ref[...] = jnp.dot(ya_ref[...], w_ref[...], preferred_element_type=F32)

    @pl.when(jnp.logical_and(k > 0, k < na))
    def _():
        o_ref[...] += jnp.dot(ya_ref[...], w_ref[...], preferred_element_type=F32)

    @pl.when(k >= na)
    def _():
        o_ref[...] += jnp.dot(yb_ref[...], w_ref[...], preferred_element_type=F32)

    @pl.when(k == pl.num_programs(1) - 1)
    def _():
        def finish_rows(rows, scale):
            o_ref[rows, :] = x_ref[rows, :] + o_ref[rows, :] * scale * g_ref[...]

        _normalise_rows(tm, o_ref, finish_rows)


def _out_proj(ya, yb, w, x, g):
    t, d = x.shape
    wa, wb = ya.shape[1], yb.shape[1]
    tm = _pick(t, (512, 256, 128))
    tk = _pick(math.gcd(wa, wb), (1024, 512, 256, 128))
    na, nb = wa // tk, wb // tk
    return pl.pallas_call(
        functools.partial(_out_proj_kernel, na),
        out_shape=jax.ShapeDtypeStruct((t, d), F32),
        grid=(t // tm, na + nb),
        in_specs=[
            pl.BlockSpec((tm, tk), lambda i, k: (i, jnp.minimum(k, na - 1))),
            pl.BlockSpec((tm, tk), lambda i, k: (i, jnp.maximum(k - na, 0))),
            pl.BlockSpec((tk, d), lambda i, k: (k, 0)),
            pl.BlockSpec((tm, d), lambda i, k: (i, 0)),
            pl.BlockSpec((1, d), lambda i, k: (0, 0)),
        ],
        out_specs=pl.BlockSpec((tm, d), lambda i, k: (i, 0)),
        compiler_params=_params(("parallel", "arbitrary")),
        name="out_proj",
    )(ya, yb, w, x, g)


def _conv_kernel(gb_ref, gc_ref, xt_ref, cw_ref, o_ref, tail_ref):
    ts = gb_ref.shape[0]
    width = cw_ref.shape[0]
    hist = tail_ref.shape[0]

    @pl.when(pl.program_id(1) == 0)
    def _():
        tail_ref[...] = jnp.zeros_like(tail_ref)

    u = gc_ref[...] * xt_ref[...]
    row = lax.broadcasted_iota(jnp.int32, u.shape, 0)
    acc = cw_ref[width - 1:width, :] * u
    for back in range(1, width):
        shifted = pltpu.roll(u, back, 0)
        for r in range(back):
            prev = tail_ref[hist - back + r:hist - back + r + 1, :]
            shifted = jnp.where(row == r, prev, shifted)
        acc = acc + cw_ref[width - 1 - back:width - back, :] * shifted
    o_ref[...] = (gb_ref[...] * acc).astype(o_ref.dtype)
    tail_ref[...] = u[ts - hist:, :]


def _short_conv(pa, conv_w, batch, seq):
    t = pa.shape[0]
    c = conv_w.shape[1]
    ts = _pick(seq, (256, 128))
    tc = _pick(c, (512, 256, 128))
    nc = c // tc
    ns = seq // ts
    return pl.pallas_call(
        _conv_kernel,
        out_shape=jax.ShapeDtypeStruct((t, c), BF16),
        grid=(batch * nc, ns),
        in_specs=[
            pl.BlockSpec((ts, tc), lambda bj, s: ((bj // nc) * ns + s, bj % nc)),
            pl.BlockSpec((ts, tc), lambda bj, s: ((bj // nc) * ns + s, nc + bj % nc)),
            pl.BlockSpec((ts, tc), lambda bj, s: ((bj // nc) * ns + s, 2 * nc + bj % nc)),
            pl.BlockSpec((conv_w.shape[0], tc), lambda bj, s: (0, bj % nc)),
        ],
        out_specs=pl.BlockSpec((ts, tc), lambda bj, s: ((bj // nc) * ns + s, bj % nc)),
        scratch_shapes=[pltpu.VMEM((8, tc), F32)],
        compiler_params=_params(("parallel", "arbitrary")),
        name="short_conv",
    )(pa, pa, pa, conv_w)


def _transposed_bf16(block):
    return block.astype(F32).T.astype(BF16)


def _init_softmax_state(m_ref, l_ref, acc_ref, slot):
    m_ref[slot] = jnp.full(m_ref.shape[1:], NEG_BIG, F32)
    l_ref[slot] = jnp.zeros(l_ref.shape[1:], F32)
    acc_ref[slot] = jnp.zeros(acc_ref.shape[1:], F32)


def _softmax_chunk(s, off, masked, v_t, m_ref, l_ref, acc_ref, slot):
    if masked:
        key = lax.broadcasted_iota(jnp.int32, s.shape, 0)
        qry = lax.broadcasted_iota(jnp.int32, s.shape, 1)
        s = jnp.where(qry >= key, s, NEG_BIG)
    m_prev = m_ref[slot]
    m_new = jnp.maximum(m_prev, jnp.max(s, axis=0, keepdims=True) + off)
    alpha = jnp.exp2(m_prev - m_new)
    p = jnp.exp2(s - (m_new - off))
    l_ref[slot] = alpha * l_ref[slot] + jnp.sum(p, axis=0, keepdims=True)
    acc_ref[slot] = alpha * acc_ref[slot] + jnp.dot(v_t, p.astype(BF16), preferred_element_type=F32)
    m_ref[slot] = m_new


def _pipelined_causal_chunks(qi, logits, update, s0_ref, s1_ref):
    logits(0, s0_ref)

    def body(p, carry):
        c = 2 * p
        logits(c + 1, s1_ref)
        update(c, s0_ref, False)
        logits(c + 2, s0_ref)
        update(c + 1, s1_ref, False)
        return carry

    lax.fori_loop(0, qi // 2, body, 0)

    @pl.when(qi % 2 == 0)
    def _():
        update(qi, s0_ref, True)

    @pl.when(qi % 2 == 1)
    def _():
        logits(qi, s1_ref)
        update(qi - 1, s0_ref, False)
        update(qi, s1_ref, True)


def _bias_columns(bias):
    lane = lax.broadcasted_iota(jnp.int32, bias.shape, 1)
    cols = jnp.zeros(bias.shape, F32)
    rest = bias
    for p in range(BIAS_PIECES):
        piece = rest.astype(BF16).astype(F32)
        cols = jnp.where(lane == p, piece, cols)
        rest = rest - piece
    return cols.astype(BF16)


def _augmented_qt(q):
    row = lax.broadcasted_iota(jnp.int32, (LANES, q.shape[0]), 0)
    ones = jnp.where(row < BIAS_PIECES, 1.0, 0.0).astype(BF16)
    return jnp.concatenate([_transposed_bf16(q), ones], axis=0)


def _diff_attn_kernel(lam_init, dh, slopes_ref, lam_ref, g_ref, q_ref, k_ref, v_ref, o_ref,
                      qt_ref, vt_ref, kb_ref, s0_ref, s1_ref, m_ref, l_ref, acc_ref):
    h = pl.program_id(1)
    qi = pl.program_id(2)
    tq = q_ref.shape[0]
    seq = k_ref.shape[0]

    slope2 = slopes_ref[h] * LOG2E

    @pl.when(qi == 0)
    def _():
        bias = lax.broadcasted_iota(jnp.int32, (tq, LANES), 0).astype(F32) * slope2
        cols = _bias_columns(bias)
        for c in range(seq // tq):
            rows = slice(c * tq, (c + 1) * tq)
            vt_ref[:, rows] = _transposed_bf16(v_ref[rows, :])
            for mp in range(2):
                kb_ref[mp, rows, :dh] = k_ref[rows, mp * dh:(mp + 1) * dh]
                kb_ref[mp, rows, dh:] = cols

    for mp in range(2):
        qt_ref[mp] = _augmented_qt(q_ref[:, mp * dh:(mp + 1) * dh])
        _init_softmax_state(m_ref, l_ref, acc_ref, mp)

    def logits(c, s_ref):
        rows = pl.ds(pl.multiple_of(c * tq, tq), tq)
        for mp in range(2):
            s_ref[mp] = jnp.dot(kb_ref[mp, rows, :], qt_ref[mp], preferred_element_type=F32)

    def update(c, s_ref, masked):
        off = ((c - qi) * tq + jnp.zeros((1, tq), jnp.int32)).astype(F32) * slope2
        v_t = vt_ref[:, pl.ds(pl.multiple_of(c * tq, tq), tq)]
        for mp in range(2):
            _softmax_chunk(s_ref[mp], off, masked, v_t, m_ref, l_ref, acc_ref, mp)

    _pipelined_causal_chunks(qi, logits, update, s0_ref, s1_ref)

    lp = lam_ref[...]
    lam = (jnp.exp(jnp.sum(lp[0:1] * lp[1:2], axis=-1, keepdims=True))
           - jnp.exp(jnp.sum(lp[2:3] * lp[3:4], axis=-1, keepdims=True)) + lam_init)
    o_t = acc_ref[0] * (1.0 / l_ref[0]) - lam * (acc_ref[1] * (1.0 / l_ref[1]))
    rms = lax.rsqrt(jnp.mean(o_t * o_t, axis=0, keepdims=True) + EPS)
    o_ref[...] = ((o_t * rms).T * g_ref[...] * (1.0 - lam_init)).astype(o_ref.dtype)


def _alibi_slopes(n):
    def pow2(m):
        start = 2.0 ** (-8.0 / m)
        return [start ** (i + 1) for i in range(m)]

    if math.log2(n).is_integer():
        s = pow2(n)
    else:
        c = 2 ** int(math.floor(math.log2(n)))
        s = pow2(c) + pow2(2 * c)[0::2][: n - c]
    return jnp.asarray(np.array(s, dtype=np.float32))


def _diff_attention(qkv, lam_p, subln_g, lam_init, batch, seq, heads, dh):
    t = qkv.shape[0]
    dv = 2 * dh
    tq = _pick(seq, (512, 256, 128))
    nq = seq // tq
    return pl.pallas_call(
        functools.partial(_diff_attn_kernel, lam_init, dh),
        out_shape=jax.ShapeDtypeStruct((t, heads * dv), BF16),
        grid=(batch, heads, nq),
        in_specs=[
            pl.BlockSpec(memory_space=pltpu.SMEM),
            pl.BlockSpec(lam_p.shape, lambda b, h, i: (0, 0)),
            pl.BlockSpec((1, dv), lambda b, h, i: (0, 0)),
            pl.BlockSpec((tq, dv), lambda b, h, i: (b * nq + i, h)),
            pl.BlockSpec((seq, dv), lambda b, h, i: (b, heads + h)),
            pl.BlockSpec((seq, dv), lambda b, h, i: (b, 2 * heads + h)),
        ],
        out_specs=pl.BlockSpec((tq, dv), lambda b, h, i: (b * nq + i, h)),
        scratch_shapes=[
            pltpu.VMEM((2, dh + LANES, tq), BF16),
            pltpu.VMEM((dv, seq), BF16),
            pltpu.VMEM((2, seq, dh + LANES), BF16),
            pltpu.VMEM((2, tq, tq), F32),
            pltpu.VMEM((2, tq, tq), F32),
            pltpu.VMEM((2, 1, tq), F32),
            pltpu.VMEM((2, 1, tq), F32),
            pltpu.VMEM((2, dv, tq), F32),
        ],
        compiler_params=_params(("parallel", "parallel", "arbitrary")),
        name="diff_attn",
    )(_alibi_slopes(heads), lam_p, subln_g, qkv, qkv, qkv)


def _fox_attn_kernel(dh, cum_ref, q_ref, k_ref, v_ref, o_ref,
                     qt_ref, vt_ref, kb_ref, first_ref, s0_ref, s1_ref, m_ref, l_ref, acc_ref):
    pair = pl.program_id(1)
    qi = pl.program_id(2)
    tq = q_ref.shape[0]
    seq = k_ref.shape[0]
    reps = tq // LANES

    @pl.when(qi == 0)
    def _():
        lane = lax.broadcasted_iota(jnp.int32, (tq, LANES), 1)
        for c in range(seq // tq):
            rows = slice(c * tq, (c + 1) * tq)
            cum = cum_ref[rows, :]
            for hh in range(2):
                vt_ref[hh, :, rows] = _transposed_bf16(v_ref[rows, hh * dh:(hh + 1) * dh])
                col = jnp.sum(jnp.where(lane == 2 * pair + hh, cum, 0.0), axis=1, keepdims=True)
                first = col[0:1, :]
                rel = jnp.broadcast_to((first - col) * LOG2E, (tq, LANES))
                kb_ref[hh, rows, :dh] = k_ref[rows, hh * dh:(hh + 1) * dh]
                kb_ref[hh, rows, dh:] = _bias_columns(rel)
                first_ref[hh, c:c + 1, :] = jnp.broadcast_to(first, (1, LANES))

    for hh in range(2):
        qt_ref[hh] = _augmented_qt(q_ref[:, hh * dh:(hh + 1) * dh])
        _init_softmax_state(m_ref, l_ref, acc_ref, hh)

    def logits(c, s_ref):
        rows = pl.ds(pl.multiple_of(c * tq, tq), tq)
        for hh in range(2):
            s_ref[hh] = jnp.dot(kb_ref[hh, rows, :], qt_ref[hh], preferred_element_type=F32)

    def update(c, s_ref, masked):
        cols = pl.ds(pl.multiple_of(c * tq, tq), tq)
        for hh in range(2):
            off = (first_ref[hh, pl.ds(qi, 1), :] - first_ref[hh, pl.ds(c, 1), :]) * LOG2E
            off = jnp.concatenate([off] * reps, axis=1)
            _softmax_chunk(s_ref[hh], off, masked, vt_ref[hh, :, cols], m_ref, l_ref, acc_ref, hh)

    _pipelined_causal_chunks(qi, logits, update, s0_ref, s1_ref)

    for hh in range(2):
        o_t = acc_ref[hh] * (1.0 / l_ref[hh])
        o_ref[:, hh * dh:(hh + 1) * dh] = o_t.T.astype(o_ref.dtype)


def _fox_attention(qkv, cum, batch, seq, heads, dh):
    t = qkv.shape[0]
    assert heads % 2 == 0
    pairs = heads // 2
    tq = _pick(seq, (512, 256, 128))
    nq = seq // tq
    return pl.pallas_call(
        functools.partial(_fox_attn_kernel, dh),
        out_shape=jax.ShapeDtypeStruct((t, heads * dh), BF16),
        grid=(batch, pairs, nq),
        in_specs=[
            pl.BlockSpec((seq, LANES), lambda b, h, i: (b, 0)),
            pl.BlockSpec((tq, 2 * dh), lambda b, h, i: (b * nq + i, h)),
            pl.BlockSpec((seq, 2 * dh), lambda b, h, i: (b, pairs + h)),
            pl.BlockSpec((seq, 2 * dh), lambda b, h, i: (b, 2 * pairs + h)),
        ],
        out_specs=pl.BlockSpec((tq, 2 * dh), lambda b, h, i: (b * nq + i, h)),
        scratch_shapes=[
            pltpu.VMEM((2, dh + LANES, tq), BF16),
            pltpu.VMEM((2, dh, seq), BF16),
            pltpu.VMEM((2, seq, dh + LANES), BF16),
            pltpu.VMEM((2, nq, LANES), F32),
            pltpu.VMEM((2, tq, tq), F32),
            pltpu.VMEM((2, tq, tq), F32),
            pltpu.VMEM((2, 1, tq), F32),
            pltpu.VMEM((2, 1, tq), F32),
            pltpu.VMEM((2, dh, tq), F32),
        ],
        compiler_params=_params(("parallel", "parallel", "arbitrary")),
        name="fox_attn",
    )(cum, qkv, qkv, qkv)


def _forget_cumsum_kernel(f_ref, b_ref, o_ref):
    seq = f_ref.shape[0]
    ck = LANES
    r = lax.broadcasted_iota(jnp.int32, (ck, ck), 0)
    c = lax.broadcasted_iota(jnp.int32, (ck, ck), 1)
    tril = (r >= c).astype(F32)

    def body(i, carry):
        rows = pl.ds(pl.multiple_of(i * ck, ck), ck)
        z = f_ref[rows, :] + b_ref[...]
        log_f = jnp.minimum(z, 0.0) - jnp.log1p(jnp.exp(-jnp.abs(z)))
        cs = jnp.dot(tril, log_f, preferred_element_type=F32, precision=lax.Precision.HIGHEST) + carry
        o_ref[rows, :] = cs
        return cs[ck - 1:ck, :]

    lax.fori_loop(0, seq // ck, body, jnp.zeros((1, ck), F32))


def _forget_cumsum(f_proj, f_bias, batch, seq):
    return pl.pallas_call(
        _forget_cumsum_kernel,
        out_shape=jax.ShapeDtypeStruct(f_proj.shape, F32),
        grid=(batch,),
        in_specs=[
            pl.BlockSpec((seq, LANES), lambda b: (b, 0)),
            pl.BlockSpec((1, LANES), lambda b: (0, 0)),
        ],
        out_specs=pl.BlockSpec((seq, LANES), lambda b: (b, 0)),
        compiler_params=_params(("parallel",)),
        name="forget_cumsum",
    )(f_proj, f_bias)


def _sgu_kernel(u_ref, z_ref, w_ref, b_ref, lng_ref, lnb_ref, o_ref):
    rows, width = u_ref.shape
    groups, ck, _ = w_ref.shape
    gw = width // groups
    r = lax.broadcasted_iota(jnp.int32, (ck, ck), 0)
    c = lax.broadcasted_iota(jnp.int32, (ck, ck), 1)
    w_tri = [jnp.where(r >= c, w_ref[g], 0.0).astype(BF16) for g in range(groups)]
    for ci in range(rows // ck):
        rs = slice(ci * ck, (ci + 1) * ck)
        z = _gelu_tanh(z_ref[rs, :])
        mu = jnp.mean(z, axis=-1, keepdims=True)
        zc = z - mu
        var = jnp.mean(zc * zc, axis=-1, keepdims=True)
        zn = (zc * lax.rsqrt(var + EPS) * lng_ref[...] + lnb_ref[...]).astype(BF16)
        for g in range(groups):
            cs = slice(g * gw, (g + 1) * gw)
            mixed = jnp.dot(w_tri[g], zn[:, cs], preferred_element_type=F32) + b_ref[g]
            o_ref[rs, cs] = (_gelu_tanh(u_ref[rs, cs]) * mixed).astype(o_ref.dtype)


def _sgu(uz, sgu_w, sgu_b, ln_g, ln_b):
    t = uz.shape[0]
    groups, ck, _ = sgu_w.shape
    width = ln_g.shape[1]
    rows = _pick(t, (2 * ck, ck))
    return pl.pallas_call(
        _sgu_kernel,
        out_shape=jax.ShapeDtypeStruct((t, width), BF16),
        grid=(t // rows,),
        in_specs=[
            pl.BlockSpec((rows, width), lambda i: (i, 0)),
            pl.BlockSpec((rows, width), lambda i: (i, 1)),
            pl.BlockSpec(sgu_w.shape, lambda i: (0, 0, 0)),
            pl.BlockSpec(sgu_b.shape, lambda i: (0, 0, 0)),
            pl.BlockSpec((1, width), lambda i: (0, 0)),
            pl.BlockSpec((1, width), lambda i: (0, 0)),
        ],
        out_specs=pl.BlockSpec((rows, width), lambda i: (i, 0)),
        compiler_params=_params(("parallel",)),
        name="sgu",
    )(uz, uz, sgu_w, sgu_b, ln_g, ln_b)


def _query_col_scale(n, q_first, n_query, dh):
    col = jnp.arange(n)
    is_query = jnp.logical_and(col >= q_first, col < q_first + n_query)
    return jnp.where(is_query, dh ** -0.5 * LOG2E, 1.0).astype(F32)[None, :]


def _even_mixer(x, g_pre, g_post, w_in, w_out, conv_w, lam_p, subln_g, layer_idx, batch, seq):
    d = x.shape[1]
    conv_ch = conv_w.shape[1]
    dh = lam_p.shape[1]
    dv = subln_g.shape[0]
    heads = (d - conv_ch) // dv
    n_in = w_in.shape[1]
    n_attn = n_in - 3 * conv_ch
    pa, qkv = _in_proj(
        x, g_pre, w_in.astype(BF16), _query_col_scale(n_in, 3 * conv_ch, 2 * heads * dh, dh),
        ((3 * conv_ch, 3 * conv_ch, F32, False), (n_attn, n_attn, BF16, True)), "even_in")
    ya = _short_conv(pa, conv_w, batch, seq)
    lam_init = 0.8 - 0.6 * math.exp(-0.3 * layer_idx)
    yb = _diff_attention(qkv, lam_p, subln_g[None, :], lam_init, batch, seq, heads, dh)
    return _out_proj(ya, yb, w_out.astype(BF16), x, g_post)


def _odd_mixer(x, g_pre, g_post, w_in, w_out, sgu_w, sgu_b, ln_g, ln_b, f_bias, batch, seq):
    d = x.shape[1]
    width = ln_g.shape[0]
    heads = f_bias.shape[0]
    dh = (d - width) // heads
    n_attn = 3 * heads * dh
    assert heads <= LANES
    gate_cols = _pick(math.gcd(2 * width, n_attn), (512, 256, 128))
    n_in = 2 * width + n_attn + gate_cols
    w_bf = jnp.pad(w_in, ((0, 0), (0, n_in - w_in.shape[1]))).astype(BF16)
    uz, qkv, f_proj = _in_proj(
        x, g_pre, w_bf, _query_col_scale(n_in, 2 * width, heads * dh, dh),
        ((2 * width, 2 * width, F32, False), (n_attn, n_attn, BF16, True), (gate_cols, LANES, F32, False)),
        "odd_in")
    bias =jnp.pad(f_bias, (0, LANES - heads))[None, :]
    cum = _forget_cumsum(f_proj, bias, batch, seq)
    yc = _sgu(uz, sgu_w, sgu_b[:, :, None], ln_g[None, :], ln_b[None, :])
    yd = _fox_attention(qkv, cum, batch, seq, heads, dh)
    return _out_proj(yc, yd, w_out.astype(BF16), x, g_post)


def kernel(x, norm_g, ffn_w_in, ffn_w_out, even_w_in, even_w_out, conv_w, diff_lambda, diff_subln_g,
           odd_w_in, odd_w_out, sgu_w, sgu_b, sgu_ln_g, sgu_ln_b, fox_f_bias):
    batch, seq, d = x.shape
    depth = norm_g.shape[0]
    xt = x.reshape(batch * seq, d)
    w_in, w_out = ffn_w_in[0, 0].astype(BF16), ffn_w_out[0, 0].astype(BF16)
    for l in range(depth):
        g = norm_g[l][:, None, :]
        xt, w_in, w_out = _ffn(xt, g[0], g[1], w_in, w_out, (ffn_w_in, ffn_w_out, l, 1))
        i = l // 2
        if l % 2 == 0:
            xt = _even_mixer(xt, g[2], g[3], even_w_in[i], even_w_out[i], conv_w[i], diff_lambda[i],
                             diff_subln_g[i], l, batch, seq)
        else:
            xt = _odd_mixer(xt, g[2], g[3], odd_w_in[i], odd_w_out[i], sgu_w[i], sgu_b[i],
                            sgu_ln_g[i], sgu_ln_b[i], fox_f_bias[i], batch, seq)
        if l + 1 < depth:
            xt, w_in, w_out = _ffn(xt, g[4], g[5], w_in, w_out, (ffn_w_in, ffn_w_out, l + 1, 0))
        else:
            (xt,) = _ffn(xt, g[4], g[5], w_in, w_out)
    return xt.reshape(batch, seq, d)
```
